```python
import jax, jax.numpy as jnp
from jax import lax
import numpy as np

D_MODEL = 1024
BATCH = 32
SEQ = 2048
DEPTH = 1
DEC_BATCH = 128
DEC_SEQ = 4
PAST_LEN = 16384
PAGE_SIZE = 128

N_META = 16
N_HEADS = 8
QK_NOPE_DIM = 64
QK_ROPE_DIM = 32
V_HEAD_DIM = 64
Q_RANK = 256
KV_RANK = 256
CONV_CH = 512
CONV_WIDTH = 31
D_FF = 2816
ROPE_THETA = 10000.0
EPS = 1e-6
Q_BLOCK = 128
ATTN_WIDTH = N_HEADS * V_HEAD_DIM
MIX_WIDTH = ATTN_WIDTH + CONV_CH
Q_HEAD_DIM = QK_NOPE_DIM + QK_ROPE_DIM
IN_COLS = Q_RANK + KV_RANK + QK_ROPE_DIM + 2 * CONV_CH
SOFTMAX_SCALE = Q_HEAD_DIM ** -0.5

kernel_name = "hybrid_mla_conformer_conv_step"


def rmsnorm(x, g):
    xf = x.astype(jnp.float32)
    y = xf * lax.rsqrt(jnp.mean(xf * xf, axis=-1, keepdims=True) + EPS)
    return (y * g.astype(jnp.float32)).astype(x.dtype)


def layernorm(x, g, b):
    xf = x.astype(jnp.float32)
    mu = jnp.mean(xf, axis=-1, keepdims=True)
    var = jnp.mean(jnp.square(xf - mu), axis=-1, keepdims=True)
    y = (xf - mu) * lax.rsqrt(var + EPS)
    return (y * g.astype(jnp.float32) + b.astype(jnp.float32)).astype(x.dtype)


def rope(x, pos):
    half = QK_ROPE_DIM // 2
    inv = ROPE_THETA ** (-jnp.arange(half, dtype=jnp.float32) / half)
    ang = pos.astype(jnp.float32)[:, None] * inv[None, :]
    cos = jnp.cos(ang)[None, :, None, :]
    sin = jnp.sin(ang)[None, :, None, :]
    xf = x.astype(jnp.float32)
    x1, x2 = xf[..., :half], xf[..., half:]
    return jnp.concatenate([x1 * cos - x2 * sin, x1 * sin + x2 * cos], axis=-1).astype(x.dtype)


def swiglu(x, w_gate, w_up, w_down):
    return (jax.nn.silu(x @ w_gate) * (x @ w_up)) @ w_down


def project_mixers(n, pos, w_in, q_norm, w_uq, kv_norm, w_uk):
    b, s, _ = n.shape
    z = n @ w_in
    q_c = z[..., :Q_RANK]
    kv_c = z[..., Q_RANK:Q_RANK + KV_RANK]
    k_r = z[..., Q_RANK + KV_RANK:Q_RANK + KV_RANK + QK_ROPE_DIM]
    conv_in = z[..., Q_RANK + KV_RANK + QK_ROPE_DIM:]
    q = (rmsnorm(q_c, q_norm) @ w_uq).reshape(b, s, N_HEADS, Q_HEAD_DIM)
    q_nope, q_rope = q[..., :QK_NOPE_DIM], rope(q[..., QK_NOPE_DIM:], pos)
    q_lat = jnp.einsum('bshd,rhd->bshr', q_nope, w_uk)
    c_kv = rmsnorm(kv_c, kv_norm)
    k_rope = rope(k_r[:, :, None, :], pos)[:, :, 0, :]
    glu = conv_in[..., :CONV_CH] * jax.nn.sigmoid(conv_in[..., CONV_CH:])
    return q_lat, q_rope, c_kv, k_rope, glu


def mla_scores(q_lat, q_rope, c, kr):
    s = jnp.einsum('bshr,btr->bhst', q_lat, c) + jnp.einsum('bshd,btd->bhst', q_rope, kr)
    return s.astype(jnp.float32) * SOFTMAX_SCALE


def prompt_attention(q_lat, q_rope, c, kr, w_uv):
    b, t = c.shape[0], c.shape[1]
    n_blk = -(-t // Q_BLOCK)
    t_pad = n_blk * Q_BLOCK
    pad = ((0, 0), (0, t_pad - t), (0, 0), (0, 0))
    qb = jnp.pad(q_lat, pad).reshape(b, n_blk, Q_BLOCK, N_HEADS, KV_RANK).swapaxes(0, 1)
    rb = jnp.pad(q_rope, pad).reshape(b, n_blk, Q_BLOCK, N_HEADS, QK_ROPE_DIM).swapaxes(0, 1)
    qpos = jnp.arange(t_pad, dtype=jnp.int32).reshape(n_blk, Q_BLOCK)
    kpos = jnp.arange(t, dtype=jnp.int32)

    def one_block(args):
        ql, qr, qp = args
        s = mla_scores(ql, qr, c, kr)
        s = jnp.where(kpos[None, :] <= qp[:, None], s, -jnp.inf)
        p = jax.nn.softmax(s, axis=-1).astype(c.dtype)
        o = jnp.einsum('bhst,btr->bshr', p, c)
        return jnp.einsum('bshr,rhv->bshv', o, w_uv).reshape(b, Q_BLOCK, ATTN_WIDTH)

    out = lax.map(one_block, (qb, rb, qpos))
    return out.swapaxes(0, 1).reshape(b, t_pad, ATTN_WIDTH)[:, :t]


def sample_attention(q_lat, q_rope, c_new, kr_new, c_past, kr_past, w_uv):
    b, s = c_new.shape[0], c_new.shape[1]
    past_len = c_past.shape[1]
    s_past = mla_scores(q_lat, q_rope, c_past, kr_past)
    s_new = mla_scores(q_lat, q_rope, c_new, kr_new)
    s_new = jnp.where(jnp.tril(jnp.ones((s, s), dtype=bool)), s_new, -jnp.inf)
    p = jax.nn.softmax(jnp.concatenate([s_past, s_new], axis=-1), axis=-1).astype(c_new.dtype)
    o = (jnp.einsum('bhst,btr->bshr', p[..., :past_len], c_past)
         + jnp.einsum('bhst,btr->bshr', p[..., past_len:], c_new))
    return jnp.einsum('bshr,rhv->bshv', o, w_uv).reshape(b, s, ATTN_WIDTH)


def conformer_conv(glu, prev, conv_w, conv_b, ln_g, ln_b):
    full = jnp.concatenate([prev, glu], axis=1)
    y = lax.conv_general_dilated(full, conv_w[:, None, :], window_strides=(1,), padding='VALID',
                                 dimension_numbers=('NWC', 'WIO', 'NWC'),
                                 feature_group_count=CONV_CH) + conv_b
    y = jax.nn.silu(layernorm(y, ln_g, ln_b))
    return y, full[:, -(CONV_WIDTH - 1):]


def layer_forward(x, pos, conv_prev, attn_fn,
                  ffn1_norm, ffn1_w_gate, ffn1_w_up, ffn1_w_down,
                  mix_norm, w_in, q_norm, w_uq, kv_norm, w_uk, w_uv,
                  conv_w, conv_b, conv_ln_g, conv_ln_b,
                  attn_grp_norm, conv_grp_norm, w_out,
                  ffn2_norm, ffn2_w_gate, ffn2_w_up, ffn2_w_down):
    h = x + 0.5 * swiglu(rmsnorm(x, ffn1_norm), ffn1_w_gate, ffn1_w_up, ffn1_w_down)
    n = rmsnorm(h, mix_norm)
    q_lat, q_rope, c_kv, k_rope, glu = project_mixers(n, pos, w_in, q_norm, w_uq, kv_norm, w_uk)
    a = attn_fn(q_lat, q_rope, c_kv, k_rope, w_uv)
    cv, conv_state = conformer_conv(glu, conv_prev, conv_w, conv_b, conv_ln_g, conv_ln_b)
    mix = jnp.concatenate([rmsnorm(a, attn_grp_norm), rmsnorm(cv, conv_grp_norm)], axis=-1) @ w_out
    h = h + mix
    h = h + 0.5 * swiglu(rmsnorm(h, ffn2_norm), ffn2_w_gate, ffn2_w_up, ffn2_w_down)
    return h, c_kv, k_rope, conv_state


def setup_inputs(seed: int = 0) -> dict:
    key = jax.random.key(seed)
    ks = iter(jax.random.split(key, 48))
    f32 = jnp.float32

    def nrm(shape, scale):
        return jax.random.normal(next(ks), shape, f32) * scale

    def gain(shape):
        return 1.0 + 0.02 * jax.random.normal(next(ks), shape, f32)

    n_pages = PAST_LEN // PAGE_SIZE
    n_used = DEC_BATCH * n_pages
    n_pool = (n_used * 5) // 4
    L = DEPTH
    d = {}
    d['x_prompt'] = nrm((BATCH, SEQ, D_MODEL), 1.0)
    d['x_sample'] = nrm((DEC_BATCH, DEC_SEQ, D_MODEL), 1.0)
    d['cache_kv_latent'] = nrm((L, n_pool, PAGE_SIZE, KV_RANK), 1.0)
    d['cache_k_rope'] = nrm((L, n_pool, PAGE_SIZE, QK_ROPE_DIM), 1.0)
    d['state_conv'] = nrm((L, DEC_BATCH, CONV_WIDTH - 1, CONV_CH), 0.5)
    perm = jax.random.permutation(next(ks), n_pool)
    d['page_table'] = perm[:n_used].reshape(DEC_BATCH, n_pages).astype(jnp.int32)
    d['meta_tokens'] = nrm((N_META, D_MODEL), 1.0)
    d['ffn1_norm'] = gain((L, D_MODEL))
    d['ffn1_w_gate'] = nrm((L, D_MODEL, D_FF), D_MODEL ** -0.5)
    d['ffn1_w_up'] = nrm((L, D_MODEL, D_FF), D_MODEL ** -0.5)
    d['ffn1_w_down'] = nrm((L, D_FF, D_MODEL), D_FF ** -0.5)
    d['mix_norm'] = gain((L, D_MODEL))
    d['w_in'] = nrm((L, D_MODEL, IN_COLS), D_MODEL ** -0.5)
    d['q_norm'] = gain((L, Q_RANK))
    d['w_uq'] = nrm((L, Q_RANK, N_HEADS * Q_HEAD_DIM), Q_RANK ** -0.5)
    d['kv_norm'] = gain((L, KV_RANK))
    d['w_uk'] = nrm((L, KV_RANK, N_HEADS, QK_NOPE_DIM), KV_RANK ** -0.5)
    d['w_uv'] = nrm((L, KV_RANK, N_HEADS, V_HEAD_DIM), KV_RANK ** -0.5)
    d['conv_w'] = nrm((L, CONV_WIDTH, CONV_CH), CONV_WIDTH ** -0.5)
    d['conv_b'] = nrm((L, CONV_CH), 0.02)
    d['conv_ln_g'] = gain((L, CONV_CH))
    d['conv_ln_b'] = nrm((L, CONV_CH), 0.02)
    d['attn_grp_norm'] = gain((L, ATTN_WIDTH))
    d['conv_grp_norm'] = gain((L, CONV_CH))
    d['w_out'] = nrm((L, MIX_WIDTH, D_MODEL), MIX_WIDTH ** -0.5)
    d['ffn2_norm'] = gain((L, D_MODEL))
    d['ffn2_w_gate'] = nrm((L, D_MODEL, D_FF), D_MODEL ** -0.5)
    d['ffn2_w_up'] = nrm((L, D_MODEL, D_FF), D_MODEL ** -0.5)
    d['ffn2_w_down'] = nrm((L, D_FF, D_MODEL), D_FF ** -0.5)
    d['final_norm'] = gain((D_MODEL,))
    return d


def reference(x_prompt, x_sample, cache_kv_latent, cache_k_rope, state_conv, page_table,
              meta_tokens, ffn1_norm, ffn1_w_gate, ffn1_w_up, ffn1_w_down,
              mix_norm, w_in, q_norm, w_uq, kv_norm, w_uk, w_uv,
              conv_w, conv_b, conv_ln_g, conv_ln_b, attn_grp_norm, conv_grp_norm, w_out,
              ffn2_norm, ffn2_w_gate, ffn2_w_up, ffn2_w_down, final_norm):
    b_p, s_p = x_prompt.shape[0], x_prompt.shape[1]
    b_s, s_s = x_sample.shape[0], x_sample.shape[1]
    past_len = page_table.shape[1] * PAGE_SIZE
    layer_params = (ffn1_norm, ffn1_w_gate, ffn1_w_up, ffn1_w_down,
                    mix_norm, w_in, q_norm, w_uq, kv_norm, w_uk, w_uv,
                    conv_w, conv_b, conv_ln_g, conv_ln_b,
                    attn_grp_norm, conv_grp_norm, w_out,
                    ffn2_norm, ffn2_w_gate, ffn2_w_up, ffn2_w_down)

    hp = jnp.concatenate([jnp.broadcast_to(meta_tokens[None].astype(x_prompt.dtype),
                                           (b_p, N_META, D_MODEL)), x_prompt], axis=1)
    pos_p = jnp.arange(s_p + N_META, dtype=jnp.int32)
    pos_s = past_len + jnp.arange(s_s, dtype=jnp.int32)
    hs = x_sample

    ckv_p, kr_p, conv_p, ckv_s, kr_s, conv_s = [], [], [], [], [], []
    for l in range(DEPTH):
        lp = [w[l] for w in layer_params]
        zeros_prev = jnp.zeros((b_p, CONV_WIDTH - 1, CONV_CH), hp.dtype)
        hp, c1, k1, st1 = layer_forward(hp, pos_p, zeros_prev, prompt_attention, *lp)

        c_past = cache_kv_latent[l, page_table].reshape(b_s, past_len, KV_RANK)
        kr_past = cache_k_rope[l, page_table].reshape(b_s, past_len, QK_ROPE_DIM)
        attn_s = lambda ql, qr, c, kr, wv: sample_attention(ql, qr, c, kr, c_past, kr_past, wv)
        hs, c2, k2, st2 = layer_forward(hs, pos_s, state_conv[l], attn_s, *lp)

        ckv_p.append(c1); kr_p.append(k1); conv_p.append(st1)
        ckv_s.append(c2); kr_s.append(k2); conv_s.append(st2)

    y_prompt = rmsnorm(hp, final_norm)[:, N_META:]
    y_sample = rmsnorm(hs, final_norm)
    return (y_prompt, y_sample,
            jnp.stack(ckv_p), jnp.stack(kr_p), jnp.stack(conv_p),
            jnp.stack(ckv_s), jnp.stack(kr_s), jnp.stack(conv_s))
```

```python
import functools
import math

import jax
import jax.numpy as jnp
from jax import lax
from jax.experimental import pallas as pl
from jax.experimental.pallas import tpu as pltpu

N_META = 16
N_HEADS = 8
QK_NOPE_DIM = 64
QK_ROPE_DIM = 32
V_HEAD_DIM = 64
Q_RANK = 256
KV_RANK = 256
CONV_CH = 512
CONV_WIDTH = 31
ROPE_THETA = 10000.0
EPS = 1e-6
PAGE_SIZE = 128
Q_HEAD_DIM = QK_NOPE_DIM + QK_ROPE_DIM
ATTN_WIDTH = N_HEADS * V_HEAD_DIM
SOFTMAX_SCALE = Q_HEAD_DIM ** -0.5
SCORE_SCALE = SOFTMAX_SCALE * math.log2(math.e)

LANES = 128
HEAD_SLOT = LANES
HALO = 32
VMEM_LIMIT = 56 * 1024 * 1024

_F32 = jnp.float32
_BF16 = jnp.bfloat16


def _const_spec(shape):
    nd = len(shape)
    return pl.BlockSpec(shape, lambda *_: (0,) * nd, pipeline_mode=pl.Buffered(1))


def _rms(x, g):
    return x * lax.rsqrt(jnp.mean(x * x, axis=-1, keepdims=True) + EPS) * g


def _sigmoid(x):
    return 1.0 / (1.0 + jnp.exp(-x))


def _dot(a, b):
    return jnp.dot(a, b, preferred_element_type=_F32)


def _dot_nt(a, b):
    return lax.dot_general(a, b, (((1,), (1,)), ((), ())), preferred_element_type=_F32)


def _ffn_kernel(x_ref, g_ref, wg_ref, wu_ref, wd_ref, fg_ref, o_ref, *, ff_chunk, final):
    x = x_ref[...]
    xn = _rms(x, g_ref[...]).astype(_BF16)
    d_ff = wg_ref.shape[1]
    acc = jnp.zeros_like(x)
    for c0 in range(0, d_ff, ff_chunk):
        gate = _dot(xn, wg_ref[:, c0:c0 + ff_chunk])
        up = _dot(xn, wu_ref[:, c0:c0 + ff_chunk])
        act = (gate * _sigmoid(gate) * up).astype(_BF16)
        acc = acc + _dot(act, wd_ref[c0:c0 + ff_chunk, :])
    y = x + 0.5 * acc
    if final:
        y = _rms(y, fg_ref[...])
    o_ref[...] = y


def _ffn(x, norm_g, wg, wu, wd, final_g, *, tm, final):
    t, d = x.shape
    d_ff = wg.shape[1]
    ff_chunk = d_ff // 2 if (d_ff // 2) % LANES == 0 else d_ff
    return pl.pallas_call(
        functools.partial(_ffn_kernel, ff_chunk=ff_chunk, final=final),
        grid=(t // tm,),
        in_specs=[
            pl.BlockSpec((tm, d), lambda i: (i, 0)),
            _const_spec((1, d)),
            _const_spec((d, d_ff)),
            _const_spec((d, d_ff)),
            _const_spec((d_ff, d)),
            _const_spec((1, d)),
        ],
        out_specs=pl.BlockSpec((tm, d), lambda i: (i, 0)),
        out_shape=jax.ShapeDtypeStruct((t, d), _F32),
        compiler_params=pltpu.CompilerParams(
            dimension_semantics=("arbitrary",), vmem_limit_bytes=VMEM_LIMIT),
        name="ffn_final" if final else "ffn",
    )(x, norm_g, wg, wu, wd, final_g)


def _proj_kernel(h_ref, cosq_ref, sinq_ref, cosk_ref, sink_ref,
                 mixg_ref, win_ref, qg_ref, wqa_ref, wqb_ref, kvg_ref, *rest, absorbed):
    if absorbed:
        wabs_ref, wqr_ref, ckv_ref, kr_ref, glu_ref, ql_ref, qr_ref = rest
    else:
        wk1_ref, wk2_ref, wv_ref, ckv_ref, kr_ref, glu_ref, q_ref, k_ref, v_ref = rest
    n = _rms(h_ref[...], mixg_ref[...]).astype(_BF16)
    z = _dot(n, win_ref[...])
    o_kv = Q_RANK
    o_ca = o_kv + KV_RANK
    o_cb = o_ca + CONV_CH
    o_ka = o_cb + CONV_CH
    o_kb = o_ka + LANES

    glu_ref[...] = z[:, o_ca:o_cb] * _sigmoid(z[:, o_cb:o_ka])

    c = _rms(z[:, o_kv:o_ca], kvg_ref[...])
    ckv_ref[...] = c
    c_bf = c.astype(_BF16)
    kr = z[:, o_ka:o_kb] * cosk_ref[...] + z[:, o_kb:o_kb + LANES] * sink_ref[...]
    kr_ref[...] = kr[:, :QK_ROPE_DIM]

    qn = _rms(z[:, :Q_RANK], qg_ref[...]).astype(_BF16)
    qa = _dot(qn, wqa_ref[...])
    qb = _dot(qn, wqb_ref[...])
    cosq = cosq_ref[...]
    sinq = sinq_ref[...]
    q_heads = []
    for h in range(N_HEADS):
        sl = slice(h * HEAD_SLOT, (h + 1) * HEAD_SLOT)
        q_heads.append((qa[:, sl] * cosq + qb[:, sl] * sinq).astype(_BF16))
    q = jnp.concatenate(q_heads, axis=-1)
    if absorbed:
        ql_ref[...] = _dot(q, wabs_ref[...]).astype(_BF16)
        qr_ref[...] = _dot(q, wqr_ref[...]).astype(_BF16)
    else:
        q_ref[...] = q
        k_ref[...] = (_dot(c_bf, wk1_ref[...]) + _dot(kr.astype(_BF16), wk2_ref[...])).astype(_BF16)
        v_ref[...] = _dot(c_bf, wv_ref[...]).astype(_BF16)


def _proj(h, tables, n_table_blocks, pw, *, tm, absorbed):
    t, d = h.shape
    cosq, sinq, cosk, sink = tables
    row = lambda i: (i, 0)
    tab = lambda i: (i % n_table_blocks, 0)
    in_specs = [
        pl.BlockSpec((tm, d), row),
        pl.BlockSpec((tm, LANES), tab), pl.BlockSpec((tm, LANES), tab),
        pl.BlockSpec((tm, LANES), tab), pl.BlockSpec((tm, LANES), tab),
        _const_spec(pw["mix_norm"].shape), _const_spec(pw["w_in"].shape),
        _const_spec(pw["q_norm"].shape), _const_spec(pw["w_qa"].shape),
        _const_spec(pw["w_qb"].shape), _const_spec(pw["kv_norm"].shape),
    ]
    args = [h, cosq, sinq, cosk, sink, pw["mix_norm"], pw["w_in"], pw["q_norm"],
            pw["w_qa"], pw["w_qb"], pw["kv_norm"]]
    out_shape = [jax.ShapeDtypeStruct((t, KV_RANK), _F32),
                 jax.ShapeDtypeStruct((t, QK_ROPE_DIM), _F32),
                 jax.ShapeDtypeStruct((t, CONV_CH), _F32)]
    out_specs = [pl.BlockSpec((tm, KV_RANK), row), pl.BlockSpec((tm, QK_ROPE_DIM), row),
                 pl.BlockSpec((tm, CONV_CH), row)]
    if absorbed:
        extra = ["w_abs", "w_qr"]
        widths = [N_HEADS * KV_RANK, N_HEADS * QK_ROPE_DIM]
    else:
        extra = ["w_k1", "w_k2", "w_v"]
        widths = [N_HEADS * HEAD_SLOT, N_HEADS * HEAD_SLOT, ATTN_WIDTH]
    for name in extra:
        in_specs.append(_const_spec(pw[name].shape))
        args.append(pw[name])
    for w in widths:
        out_shape.append(jax.ShapeDtypeStruct((t, w), _BF16))
        out_specs.append(pl.BlockSpec((tm, w), row))
    return pl.pallas_call(
        functools.partial(_proj_kernel, absorbed=absorbed),
        grid=(t // tm,),
        in_specs=in_specs,
        out_specs=out_specs,
        out_shape=out_shape,
        compiler_params=pltpu.CompilerParams(
            dimension_semantics=("arbitrary",), vmem_limit_bytes=VMEM_LIMIT),
        name="proj_absorbed" if absorbed else "proj",
    )(*args)


def _softmax_step(q_h, k, v, m, l, acc, mask=None):
    s = _dot_nt(q_h, k)
    if mask is not None:
        s = jnp.where(mask, s, -jnp.inf)
    m_new = jnp.maximum(m, jnp.max(s, axis=-1, keepdims=True))
    alpha = jnp.exp2(m - m_new)
    p = jnp.exp2(s - m_new)
    l = l * alpha + jnp.sum(p, axis=-1, keepdims=True)
    acc = acc * alpha + _dot(p.astype(_BF16), v)
    return m_new, l, acc


def _prompt_attn_kernel(q_ref, k_ref, v_ref, km_ref, vm_ref, g_ref, o_ref, *, tq, tk):
    i = pl.program_id(1)
    n_full = i * (tq // tk)
    lane = lax.broadcasted_iota(jnp.int32, (tq, LANES), 1)
    meta_mask = lane < N_META
    row = lax.broadcasted_iota(jnp.int32, (tq, tk), 0)
    col = lax.broadcasted_iota(jnp.int32, (tq, tk), 1)
    low_half = lane < V_HEAD_DIM
    pair_out = []
    head_out = [None, None]
    for h in range(N_HEADS):
        qs = slice(h * HEAD_SLOT, (h + 1) * HEAD_SLOT)
        vs = slice((h // 2) * LANES, (h // 2 + 1) * LANES)
        q_h = q_ref[:, qs]
        m0 = jnp.full((tq, 1), -jnp.inf, _F32)
        l0 = jnp.zeros((tq, 1), _F32)
        a0 = jnp.zeros((tq, LANES), _F32)
        carry = _softmax_step(q_h, km_ref[:, qs], vm_ref[:, vs], m0, l0, a0, meta_mask)

        def full_step(j, carry):
            ks = pl.ds(pl.multiple_of(j * tk, tk), tk)
            return _softmax_step(q_h, k_ref[ks, qs], v_ref[ks, vs], *carry)

        carry = lax.fori_loop(0, n_full, full_step, carry)
        for d in range(tq // tk):
            ks = pl.ds(pl.multiple_of((n_full + d) * tk, tk), tk)
            carry = _softmax_step(q_h, k_ref[ks, qs], v_ref[ks, vs], *carry,
                                  mask=col + d * tk <= row)
        m, l, acc = carry
        head_out[h % 2] = acc * (1.0 / l)
        if h % 2 == 1:
            pair_out.append(jnp.where(low_half, head_out[0], head_out[1]))
    a = jnp.concatenate(pair_out, axis=-1)
    o_ref[...] = _rms(a, g_ref[...]).astype(_BF16)


def _prompt_attention(q, k, v, k_meta, v_meta, grp_g, *, batch, seq, tq, tk):
    nq = seq // tq
    kw = N_HEADS * HEAD_SLOT
    return pl.pallas_call(
        functools.partial(_prompt_attn_kernel, tq=tq, tk=tk),
        grid=(batch, nq),
        in_specs=[
            pl.BlockSpec((tq, kw), lambda b, i: (b * nq + i, 0)),
            pl.BlockSpec((seq, kw), lambda b, i: (b, 0)),
            pl.BlockSpec((seq, ATTN_WIDTH), lambda b, i: (b, 0)),
            _const_spec((LANES, kw)),
            _const_spec((LANES, ATTN_WIDTH)),
            _const_spec((1, ATTN_WIDTH)),
        ],
        out_specs=pl.BlockSpec((tq, ATTN_WIDTH), lambda b, i: (b * nq + i, 0)),
        out_shape=jax.ShapeDtypeStruct((batch * seq, ATTN_WIDTH), _BF16),
        compiler_params=pltpu.CompilerParams(
            dimension_semantics=("arbitrary", "arbitrary"), vmem_limit_bytes=VMEM_LIMIT),
        name="prompt_attn",
    )(q, k, v, k_meta, v_meta, grp_g)


def _sample_attn_kernel(pt_ref, ql_ref, qr_ref, cn_ref, kn_ref, wv_ref, g_ref, ckv_hbm, kr_hbm,
                        o_ref, cbuf, kbuf, sem, m_sc, l_sc, acc_sc, *, pages_per_step, n_steps, s_new):
    b = pl.program_id(0)
    j = pl.program_id(1)
    nb = pl.num_programs(0)
    step = b * n_steps + j
    slot = step % 2
    rows = s_new * N_HEADS

    def page_copies(bb, jj, sl):
        copies = []
        for p in range(pages_per_step):
            page = pt_ref[bb, jj * pages_per_step + p]
            copies.append(pltpu.make_async_copy(ckv_hbm.at[page], cbuf.at[sl, p], sem.at[0, sl]))
            copies.append(pltpu.make_async_copy(kr_hbm.at[page], kbuf.at[sl, p], sem.at[1, sl]))
        return copies

    @pl.when(step == 0)
    def _():
        for cp in page_copies(0, 0, 0):
            cp.start()

    @pl.when(step + 1 < nb * n_steps)
    def _():
        nxt = step + 1
        for cp in page_copies(nxt // n_steps, nxt % n_steps, 1 - slot):
            cp.start()

    @pl.when(j == 0)
    def _():
        m_sc[...] = jnp.full(m_sc.shape, -jnp.inf, _F32)
        l_sc[...] = jnp.zeros(l_sc.shape, _F32)
        acc_sc[...] = jnp.zeros(acc_sc.shape, _F32)

    for cp in page_copies(b, j, slot):
        cp.wait()

    ql = ql_ref[0]
    qr = qr_ref[0]
    n_keys = pages_per_step * PAGE_SIZE
    c_bf = cbuf[slot].reshape(n_keys, KV_RANK).astype(_BF16)
    k_bf = kbuf[slot].reshape(n_keys, QK_ROPE_DIM).astype(_BF16)
    s = _dot_nt(ql, c_bf) + _dot_nt(qr, k_bf)
    m = m_sc[...]
    m_new = jnp.maximum(m, jnp.max(s, axis=-1, keepdims=True))
    alpha = jnp.exp2(m - m_new)
    p = jnp.exp2(s - m_new)
    l_sc[...] = l_sc[...] * alpha + jnp.sum(p, axis=-1, keepdims=True)
    acc_sc[...] = acc_sc[...] * alpha + _dot(p.astype(_BF16), c_bf)
    m_sc[...] = m_new

    @pl.when(j == n_steps - 1)
    def _():
        qlf = ql.astype(_F32)
        qrf = qr.astype(_F32)
        tok = lax.broadcasted_iota(jnp.int32, (rows, 1), 0) // N_HEADS
        cn = cn_ref[0].astype(_BF16).astype(_F32)
        kn = kn_ref[0].astype(_BF16).astype(_F32)
        s_cols = []
        for t in range(s_new):
            st = (jnp.sum(qlf * cn[t:t + 1, :], axis=-1, keepdims=True)
                  + jnp.sum(qrf * kn[t:t + 1, :], axis=-1, keepdims=True))
            s_cols.append(jnp.where(tok >= t, st, -jnp.inf))
        m = m_sc[...]
        m_new = m
        for st in s_cols:
            m_new = jnp.maximum(m_new, st)
        alpha = jnp.exp2(m - m_new)
        l = l_sc[...] * alpha
        acc = acc_sc[...] * alpha
        for t in range(s_new):
            pt = jnp.exp2(s_cols[t] - m_new)
            l = l + pt
            acc = acc + pt.astype(_BF16).astype(_F32) * cn[t:t + 1, :]
        o = (acc * (1.0 / l)).astype(_BF16)
        full = _dot(o, wv_ref[...])
        head = lax.broadcasted_iota(jnp.int32, (rows, ATTN_WIDTH), 0) % N_HEADS
        lane_head = lax.broadcasted_iota(jnp.int32, (rows, ATTN_WIDTH), 1) // V_HEAD_DIM
        full = jnp.where(head == lane_head, full, 0.0)
        a = jnp.concatenate(
            [jnp.sum(full[t * N_HEADS:(t + 1) * N_HEADS], axis=0, keepdims=True) for t in range(s_new)],
            axis=0)
        o_ref[0] = _rms(a, g_ref[...]).astype(_BF16)


def _sample_attention(page_table, ql, qr, c_new, k_new, w_v, grp_g, cache_ckv, cache_kr, *, pages_per_step):
    nb, n_pages = page_table.shape
    n_steps = n_pages // pages_per_step
    s_new = c_new.shape[1]
    rows = s_new * N_HEADS
    grid_spec = pltpu.PrefetchScalarGridSpec(
        num_scalar_prefetch=1,
        grid=(nb, n_steps),
        in_specs=[
            pl.BlockSpec((1, rows, KV_RANK), lambda b, j, pt: (b, 0, 0)),
            pl.BlockSpec((1, rows, QK_ROPE_DIM), lambda b, j, pt: (b, 0, 0)),
            pl.BlockSpec((1, s_new, KV_RANK), lambda b, j, pt: (b, 0, 0)),
            pl.BlockSpec((1, s_new, QK_ROPE_DIM), lambda b, j, pt: (b, 0, 0)),
            _const_spec((KV_RANK, ATTN_WIDTH)),
            _const_spec((1, ATTN_WIDTH)),
            pl.BlockSpec(memory_space=pl.ANY),
            pl.BlockSpec(memory_space=pl.ANY),
        ],
        out_specs=pl.BlockSpec((1, s_new, ATTN_WIDTH), lambda b, j, pt: (b, 0, 0)),
        scratch_shapes=[
            pltpu.VMEM((2, pages_per_step, PAGE_SIZE, KV_RANK), _F32),
            pltpu.VMEM((2, pages_per_step, PAGE_SIZE, QK_ROPE_DIM), _F32),
            pltpu.SemaphoreType.DMA((2, 2)),
            pltpu.VMEM((rows, 1), _F32),
            pltpu.VMEM((rows, 1), _F32),
            pltpu.VMEM((rows, KV_RANK), _F32),
        ],
    )
    return pl.pallas_call(
        functools.partial(_sample_attn_kernel, pages_per_step=pages_per_step, n_steps=n_steps, s_new=s_new),
        grid_spec=grid_spec,
        out_shape=jax.ShapeDtypeStruct((nb, s_new, ATTN_WIDTH), _BF16),
        compiler_params=pltpu.CompilerParams(
            dimension_semantics=("arbitrary", "arbitrary"), vmem_limit_bytes=VMEM_LIMIT),
        name="sample_attn",
    )(page_table, ql, qr, c_new, k_new, w_v, grp_g, cache_ckv, cache_kr)


def _mix_tail(y, h, a_n, cb_ref, lng_ref, lnb_ref, cg_ref, woa_ref, woc_ref):
    y = y + cb_ref[...]
    mu = jnp.mean(y, axis=-1, keepdims=True)
    yc = y - mu
    var = jnp.mean(yc * yc, axis=-1, keepdims=True)
    y = yc * lax.rsqrt(var + EPS) * lng_ref[...] + lnb_ref[...]
    y = y * _sigmoid(y)
    cv = _rms(y, cg_ref[...]).astype(_BF16)
    return h + _dot(a_n, woa_ref[...]) + _dot(cv, woc_ref[...])


def _mix_prompt_kernel(h_ref, a_ref, glu_ref, prev_ref, first_ref, cw_ref, cb_ref, lng_ref, lnb_ref,
                       cg_ref, woa_ref, woc_ref, o_ref, ext_sc, *, tm):
    i = pl.program_id(1)
    ext_sc[:HALO, :] = jnp.where(i == 0, first_ref[...], prev_ref[...])
    ext_sc[HALO:, :] = glu_ref[...]
    off = HALO - (CONV_WIDTH - 1)
    y = jnp.zeros((tm, CONV_CH), _F32)
    for k in range(CONV_WIDTH):
        y = y + ext_sc[off + k:off + k + tm, :] * cw_ref[k:k + 1, :]
    o_ref[...] = _mix_tail(y, h_ref[...], a_ref[...], cb_ref, lng_ref, lnb_ref, cg_ref, woa_ref, woc_ref)


def _mix_prompt(h, a_n, glu, first_halo, mw, *, batch, seq, tm):
    nt = seq // tm
    d = h.shape[1]
    hb = tm // HALO
    row = lambda b, i: (b * nt + i, 0)
    return pl.pallas_call(
        functools.partial(_mix_prompt_kernel, tm=tm),
        grid=(batch, nt),
        in_specs=[
            pl.BlockSpec((tm, d), row),
            pl.BlockSpec((tm, ATTN_WIDTH), row),
            pl.BlockSpec((tm, CONV_CH), row),
            pl.BlockSpec((HALO, CONV_CH), lambda b, i: (jnp.maximum((b * nt + i) * hb - 1, 0), 0)),
            _const_spec((HALO, CONV_CH)),
            _const_spec(mw["conv_w"].shape), _const_spec(mw["conv_b"].shape),
            _const_spec(mw["ln_g"].shape), _const_spec(mw["ln_b"].shape),
            _const_spec(mw["conv_grp"].shape),
            _const_spec(mw["w_out_a"].shape), _const_spec(mw["w_out_c"].shape),
        ],
        out_specs=pl.BlockSpec((tm, d), row),
        out_shape=jax.ShapeDtypeStruct(h.shape, _F32),
        scratch_shapes=[pltpu.VMEM((HALO + tm, CONV_CH), _F32)],
        compiler_params=pltpu.CompilerParams(
            dimension_semantics=("arbitrary", "arbitrary"), vmem_limit_bytes=VMEM_LIMIT),
        name="mix_prompt",
    )(h, a_n, glu, glu, first_halo, mw["conv_w"], mw["conv_b"], mw["ln_g"], mw["ln_b"],
      mw["conv_grp"], mw["w_out_a"], mw["w_out_c"])


def _mix_sample_kernel(h_ref, a_ref, full_ref, cw_ref, cb_ref, lng_ref, lnb_ref,
                       cg_ref, woa_ref, woc_ref, o_ref, *, s_new):
    ys = []
    for t in range(s_new):
        y = jnp.zeros(full_ref.shape[1:], _F32)
        for k in range(CONV_WIDTH):
            y = y + full_ref[t + k] * cw_ref[k:k + 1, :]
        ys.append(y)
    y = jnp.concatenate(ys, axis=0)
    o_ref[...] = _mix_tail(y, h_ref[...], a_ref[...], cb_ref, lng_ref, lnb_ref, cg_ref, woa_ref, woc_ref)


def _mix_sample(h, a_n, full_t, mw, *, s_new):
    t, d = h.shape
    return pl.pallas_call(
        functools.partial(_mix_sample_kernel, s_new=s_new),
        grid=(1,),
        in_specs=[
            _const_spec((t, d)), _const_spec((t, ATTN_WIDTH)), _const_spec(full_t.shape),
            _const_spec(mw["conv_w"].shape), _const_spec(mw["conv_b"].shape),
            _const_spec(mw["ln_g"].shape), _const_spec(mw["ln_b"].shape),
            _const_spec(mw["conv_grp"].shape),
            _const_spec(mw["w_out_a"].shape), _const_spec(mw["w_out_c"].shape),
        ],
        out_specs=pl.BlockSpec((t, d), lambda i: (0, 0)),
        out_shape=jax.ShapeDtypeStruct((t, d), _F32),
        compiler_params=pltpu.CompilerParams(
            dimension_semantics=("arbitrary",), vmem_limit_bytes=VMEM_LIMIT),
        name="mix_sample",
    )(h, a_n, full_t, mw["conv_w"], mw["conv_b"], mw["ln_g"], mw["ln_b"],
      mw["conv_grp"], mw["w_out_a"], mw["w_out_c"])


def _rope_tables(pos):
    half = QK_ROPE_DIM // 2
    inv = ROPE_THETA ** (-jnp.arange(half, dtype=_F32) / half)
    ang = pos.astype(_F32)[:, None] * inv[None, :]
    cos = jnp.cos(ang)
    sin = jnp.sin(ang)
    n = pos.shape[0]
    cos2 = jnp.concatenate([cos, cos], axis=-1)
    sin2 = jnp.concatenate([sin, sin], axis=-1)
    pad_q = jnp.zeros((n, HEAD_SLOT - Q_HEAD_DIM), _F32)
    cosq = SCORE_SCALE * jnp.concatenate([jnp.ones((n, QK_NOPE_DIM), _F32), cos2, pad_q], axis=-1)
    sinq = SCORE_SCALE * jnp.concatenate([jnp.zeros((n, QK_NOPE_DIM), _F32), sin2, pad_q], axis=-1)
    pad_k = jnp.zeros((n, LANES - QK_ROPE_DIM), _F32)
    cosk = jnp.concatenate([cos2, pad_k], axis=-1)
    sink = jnp.concatenate([sin2, pad_k], axis=-1)
    return cosq, sinq, cosk, sink


def _rot_cols(w):
    half = QK_ROPE_DIM // 2
    return jnp.concatenate([-w[..., half:], w[..., :half]], axis=-1)


def _layer_weights(l, mix_norm, w_in, q_norm, w_uq, kv_norm, w_uk, w_uv):
    d = w_in.shape[1]
    win = w_in[l]
    o_kv = Q_RANK
    o_kr = o_kv + KV_RANK
    o_cv = o_kr + QK_ROPE_DIM
    w_kr = win[:, o_kr:o_cv]
    zpad = jnp.zeros((d, LANES - QK_ROPE_DIM), _F32)
    win_p = jnp.concatenate([win[:, :o_kr], win[:, o_cv:], w_kr, zpad, _rot_cols(w_kr), zpad], axis=-1)

    wq = w_uq[l].reshape(Q_RANK, N_HEADS, Q_HEAD_DIM)
    zq = jnp.zeros((Q_RANK, N_HEADS, HEAD_SLOT - Q_HEAD_DIM), _F32)
    w_qa = jnp.concatenate([wq, zq], axis=-1).reshape(Q_RANK, N_HEADS * HEAD_SLOT)
    w_qb = jnp.concatenate([jnp.zeros((Q_RANK, N_HEADS, QK_NOPE_DIM), _F32),
                            _rot_cols(wq[..., QK_NOPE_DIM:]), zq], axis=-1).reshape(Q_RANK, N_HEADS * HEAD_SLOT)

    wuk = w_uk[l]
    zk = jnp.zeros((KV_RANK, N_HEADS, HEAD_SLOT - QK_NOPE_DIM), _F32)
    w_k1 = jnp.concatenate([wuk, zk], axis=-1).reshape(KV_RANK, N_HEADS * HEAD_SLOT)
    eye = jnp.eye(QK_ROPE_DIM, dtype=_F32)
    place = jnp.concatenate([jnp.zeros((QK_ROPE_DIM, QK_NOPE_DIM), _F32), eye,
                             jnp.zeros((QK_ROPE_DIM, HEAD_SLOT - Q_HEAD_DIM), _F32)], axis=-1)
    w_k2 = jnp.concatenate([jnp.tile(place, (1, N_HEADS)),
                            jnp.zeros((LANES - QK_ROPE_DIM, N_HEADS * HEAD_SLOT), _F32)], axis=0)
    w_v = w_uv[l].reshape(KV_RANK, ATTN_WIDTH)

    w_abs = jnp.zeros((N_HEADS, HEAD_SLOT, N_HEADS, KV_RANK), _F32)
    w_qr = jnp.zeros((N_HEADS, HEAD_SLOT, N_HEADS, QK_ROPE_DIM), _F32)
    for h in range(N_HEADS):
        w_abs = w_abs.at[h, :QK_NOPE_DIM, h, :].set(wuk[:, h, :].T)
        w_qr = w_qr.at[h, QK_NOPE_DIM:Q_HEAD_DIM, h, :].set(eye)
    w_abs = w_abs.reshape(N_HEADS * HEAD_SLOT, N_HEADS * KV_RANK)
    w_qr = w_qr.reshape(N_HEADS * HEAD_SLOT, N_HEADS * QK_ROPE_DIM)

    bf = lambda x: x.astype(_BF16)
    return {
        "mix_norm": mix_norm[l][None], "w_in": bf(win_p), "q_norm": q_norm[l][None],
        "w_qa": bf(w_qa), "w_qb": bf(w_qb), "kv_norm": kv_norm[l][None],
        "w_k1": bf(w_k1), "w_k2": bf(w_k2), "w_v": bf(w_v), "w_abs": bf(w_abs), "w_qr": bf(w_qr),
    }


def _pick_tile(n, prefer):
    for t in prefer:
        if n % t == 0:
            return t
    return n


def kernel(x_prompt, x_sample, cache_kv_latent, cache_k_rope, state_conv, page_table, meta_tokens,
           ffn1_norm, ffn1_w_gate, ffn1_w_up, ffn1_w_down, mix_norm, w_in, q_norm, w_uq, kv_norm,
           w_uk, w_uv, conv_w, conv_b, conv_ln_g, conv_ln_b, attn_grp_norm, conv_grp_norm, w_out,
           ffn2_norm, ffn2_w_gate, ffn2_w_up, ffn2_w_down, final_norm):
    b_p, s_p, d = x_prompt.shape
    b_s, s_s, _ = x_sample.shape
    depth = ffn1_norm.shape[0]
    assert depth == 1, "the meta rows' later-layer inputs are not carried; single layer only"
    n_pages = page_table.shape[1]
    past_len = n_pages * PAGE_SIZE
    assert N_META + (CONV_WIDTH - 1) - N_META <= HALO and s_p % HALO == 0
    l = 0
    bf = lambda x: x.astype(_BF16)

    pw = _layer_weights(l, mix_norm, w_in, q_norm, w_uq, kv_norm, w_uk, w_uv)
    f1 = (ffn1_norm[l][None], bf(ffn1_w_gate[l]), bf(ffn1_w_up[l]), bf(ffn1_w_down[l]))
    f2 = (ffn2_norm[l][None], bf(ffn2_w_gate[l]), bf(ffn2_w_up[l]), bf(ffn2_w_down[l]))
    fin = final_norm[None]
    mw = {
        "conv_w": jnp.concatenate([conv_w[l], jnp.zeros((1, CONV_CH), _F32)], axis=0),
        "conv_b": conv_b[l][None], "ln_g": conv_ln_g[l][None], "ln_b": conv_ln_b[l][None],
        "conv_grp": conv_grp_norm[l][None],
        "w_out_a": bf(w_out[l][:ATTN_WIDTH]), "w_out_c": bf(w_out[l][ATTN_WIDTH:]),
    }
    attn_g = attn_grp_norm[l][None]

    tm_p = _pick_tile(s_p, (512, 256, 128))
    tq = _pick_tile(s_p, (512, 256, 128))
    tk = _pick_tile(tq, (256, 128))

    h_m = _ffn(meta_tokens, *f1, fin, tm=N_META, final=False)
    tab_m = _rope_tables(jnp.arange(N_META, dtype=jnp.int32))
    ckv_m, kr_m, glu_m, _, k_m, v_m = _proj(h_m, tab_m, 1, pw, tm=N_META, absorbed=False)
    pad_rows = lambda x: jnp.concatenate([x, jnp.zeros((LANES - N_META, x.shape[1]), x.dtype)], axis=0)
    first_halo = jnp.concatenate([jnp.zeros((HALO - N_META, CONV_CH), _F32), glu_m], axis=0)

    xp = x_prompt.reshape(b_p * s_p, d)
    h_p = _ffn(xp, *f1, fin, tm=tm_p, final=False)
    tab_p = _rope_tables(N_META + jnp.arange(s_p, dtype=jnp.int32))
    ckv_p, kr_p, glu_p, q_p, k_p, v_p = _proj(h_p, tab_p, s_p // tm_p, pw, tm=tm_p, absorbed=False)
    a_p = _prompt_attention(q_p, k_p, v_p, pad_rows(k_m), pad_rows(v_m), attn_g,
                            batch=b_p, seq=s_p, tq=tq, tk=tk)
    h2_p = _mix_prompt(h_p, a_p, glu_p, first_halo, mw, batch=b_p, seq=s_p, tm=tm_p)
    y_p = _ffn(h2_p, *f2, fin, tm=tm_p, final=True).reshape(b_p, s_p, d)

    n_s = b_s * s_s
    xs = x_sample.transpose(1, 0, 2).reshape(n_s, d)
    h_s = _ffn(xs, *f1, fin, tm=n_s, final=False)
    pos_s = past_len + jnp.repeat(jnp.arange(s_s, dtype=jnp.int32), b_s)
    ckv_s, kr_s, glu_s, ql_s, qr_s = _proj(h_s, _rope_tables(pos_s), 1, pw, tm=n_s, absorbed=True)
    seq_major = lambda x: x.reshape(s_s, b_s, -1).transpose(1, 0, 2)
    ckv_sb, kr_sb, glu_sb = seq_major(ckv_s), seq_major(kr_s), seq_major(glu_s)
    ql_b = seq_major(ql_s).reshape(b_s, s_s * N_HEADS, KV_RANK)
    qr_b = seq_major(qr_s).reshape(b_s, s_s * N_HEADS, QK_ROPE_DIM)
    a_s = _sample_attention(page_table, ql_b, qr_b, ckv_sb, kr_sb, pw["w_v"], attn_g,
                            cache_kv_latent[l], cache_k_rope[l],
                            pages_per_step=_pick_tile(n_pages, (16, 8, 4, 2)))
    a_st = a_s.transpose(1, 0, 2).reshape(n_s, ATTN_WIDTH)
    full_t = jnp.concatenate([state_conv[l].transpose(1, 0, 2), glu_s.reshape(s_s, b_s, CONV_CH)], axis=0)
    h2_s = _mix_sample(h_s, a_st, full_t, mw, s_new=s_s)
    y_s = _ffn(h2_s, *f2, fin, tm=n_s, final=True).reshape(s_s, b_s, d).transpose(1, 0, 2)

    bcast = lambda x: jnp.broadcast_to(x[None], (b_p,) + x.shape)
    new_ckv_p = jnp.concatenate([bcast(ckv_m), ckv_p.reshape(b_p, s_p, KV_RANK)], axis=1)[None]
    new_kr_p = jnp.concatenate([bcast(kr_m), kr_p.reshape(b_p, s_p, QK_ROPE_DIM)], axis=1)[None]
    n_keep = CONV_WIDTH - 1
    new_conv_p = glu_p.reshape(b_p, s_p, CONV_CH)[:, s_p - n_keep:][None]
    new_conv_s = jnp.concatenate([state_conv[l], glu_sb], axis=1)[:, -n_keep:][None]
    return (y_p, y_s, new_ckv_p, new_kr_p, new_conv_p, ckv_sb[None], kr_sb[None], new_conv_s)
```

```python
import functools
import math

import jax
import jax.numpy as jnp
from jax import lax
from jax.experimental import pallas as pl
from jax.experimental.pallas import tpu as pltpu

N_META = 16
N_HEADS = 8
QK_NOPE_DIM = 64
QK_ROPE_DIM = 32
V_HEAD_DIM = 64
Q_RANK = 256
KV_RANK = 256
CONV_CH = 512
CONV_WIDTH = 31
ROPE_THETA = 10000.0
EPS = 1e-6
PAGE_SIZE = 128
Q_HEAD_DIM = QK_NOPE_DIM + QK_ROPE_DIM
ATTN_WIDTH = N_HEADS * V_HEAD_DIM
SOFTMAX_SCALE = Q_HEAD_DIM ** -0.5
SCORE_SCALE = SOFTMAX_SCALE * math.log2(math.e)

LANES = 128
SUBLANES = 8
HEAD_SLOT = LANES
HALO = 32
CONV_ROWS = 128
HEAD_LOOKAHEAD = 2
VMEM_LIMIT = 56 * 1024 * 1024

_F32 = jnp.float32
_BF16 = jnp.bfloat16


def _const_spec(shape):
    nd = len(shape)
    return pl.BlockSpec(shape, lambda *_: (0,) * nd, pipeline_mode=pl.Buffered(1))


def _rms(x, g):
    return x * lax.rsqrt(jnp.mean(x * x, axis=-1, keepdims=True) + EPS) * g


def _sigmoid(x):
    return 1.0 / (1.0 + jnp.exp(-x))


def _dot(a, b):
    return jnp.dot(a, b, preferred_element_type=_F32)


def _dot_nt(a, b):
    return lax.dot_general(a, b, (((1,), (1,)), ((), ())), preferred_element_type=_F32)


def _ffn_kernel(x_ref, g_ref, wg_ref, wu_ref, wd_ref, fg_ref, o_ref, *, ff_chunk, final):
    x = x_ref[...]
    xn = _rms(x, g_ref[...]).astype(_BF16)
    d_ff = wg_ref.shape[1]
    acc = jnp.zeros_like(x)
    for c0 in range(0, d_ff, ff_chunk):
        gate = _dot(xn, wg_ref[:, c0:c0 + ff_chunk])
        up = _dot(xn, wu_ref[:, c0:c0 + ff_chunk])
        act = (gate * _sigmoid(gate) * up).astype(_BF16)
        acc = acc + _dot(act, wd_ref[c0:c0 + ff_chunk, :])
    y = x + 0.5 * acc
    if final:
        y = _rms(y, fg_ref[...])
    o_ref[...] = y


def _ffn(x, norm_g, wg, wu, wd, final_g, *, tm, final):
    t, d = x.shape
    d_ff = wg.shape[1]
    ff_chunk = d_ff // 2 if (d_ff // 2) % LANES == 0 else d_ff
    return pl.pallas_call(
        functools.partial(_ffn_kernel, ff_chunk=ff_chunk, final=final),
        grid=(t // tm,),
        in_specs=[
            pl.BlockSpec((tm, d), lambda i: (i, 0)),
            _const_spec((1, d)),
            _const_spec((d, d_ff)),
            _const_spec((d, d_ff)),
            _const_spec((d_ff, d)),
            _const_spec((1, d)),
        ],
        out_specs=pl.BlockSpec((tm, d), lambda i: (i, 0)),
        out_shape=jax.ShapeDtypeStruct((t, d), _F32),
        compiler_params=pltpu.CompilerParams(
            dimension_semantics=("arbitrary",), vmem_limit_bytes=VMEM_LIMIT),
        name="ffn_final" if final else "ffn",
    )(x, norm_g, wg, wu, wd, final_g)


def _proj_kernel(h_ref, cosq_ref, sinq_ref, cosk_ref, sink_ref,
                 mixg_ref, win_ref, qg_ref, wqa_ref, wqb_ref, kvg_ref, *rest, absorbed):
    if absorbed:
        wabs_ref, wqr_ref, ckv_ref, kr_ref, glu_ref, ql_ref, qr_ref = rest
    else:
        wk1_ref, wk2_ref, wvt_ref, ckv_ref, kr_ref, glu_ref, q_ref, k_ref, vt_ref = rest
    n = _rms(h_ref[...], mixg_ref[...]).astype(_BF16)
    z = _dot(n, win_ref[...])
    o_kv = Q_RANK
    o_ca = o_kv + KV_RANK
    o_cb = o_ca + CONV_CH
    o_ka = o_cb + CONV_CH
    o_kb = o_ka + LANES

    glu_ref[...] = z[:, o_ca:o_cb] * _sigmoid(z[:, o_cb:o_ka])

    c = _rms(z[:, o_kv:o_ca], kvg_ref[...])
    ckv_ref[...] = c
    c_bf = c.astype(_BF16)
    kr = z[:, o_ka:o_kb] * cosk_ref[...] + z[:, o_kb:o_kb + LANES] * sink_ref[...]
    kr_ref[...] = kr[:, :QK_ROPE_DIM]

    qn = _rms(z[:, :Q_RANK], qg_ref[...]).astype(_BF16)
    qa = _dot(qn, wqa_ref[...])
    qb = _dot(qn, wqb_ref[...])
    cosq = cosq_ref[...]
    sinq = sinq_ref[...]
    q_heads = []
    for h in range(N_HEADS):
        sl = slice(h * HEAD_SLOT, (h + 1) * HEAD_SLOT)
        q_heads.append((qa[:, sl] * cosq + qb[:, sl] * sinq).astype(_BF16))
    q = jnp.concatenate(q_heads, axis=-1)
    if absorbed:
        ql_ref[...] = _dot(q, wabs_ref[...]).astype(_BF16)
        qr_ref[...] = _dot(q, wqr_ref[...]).astype(_BF16)
    else:
        q_ref[...] = q
        k_ref[...] = (_dot(c_bf, wk1_ref[...]) + _dot(kr.astype(_BF16), wk2_ref[...])).astype(_BF16)
        vt = _dot_nt(wvt_ref[...], c_bf).astype(_BF16)
        tkv = vt_ref.shape[2]
        for blk in range(vt_ref.shape[0]):
            vt_ref[blk] = vt[:, blk * tkv:(blk + 1) * tkv]


def _proj(h, tables, n_table_blocks, pw, *, tm, absorbed, tkv=None):
    t, d = h.shape
    cosq, sinq, cosk, sink = tables
    row = lambda i: (i, 0)
    tab = lambda i: (i % n_table_blocks, 0)
    in_specs = [
        pl.BlockSpec((tm, d), row),
        pl.BlockSpec((tm, LANES), tab), pl.BlockSpec((tm, LANES), tab),
        pl.BlockSpec((tm, LANES), tab), pl.BlockSpec((tm, LANES), tab),
        _const_spec(pw["mix_norm"].shape), _const_spec(pw["w_in"].shape),
        _const_spec(pw["q_norm"].shape), _const_spec(pw["w_qa"].shape),
        _const_spec(pw["w_qb"].shape), _const_spec(pw["kv_norm"].shape),
    ]
    args = [h, cosq, sinq, cosk, sink, pw["mix_norm"], pw["w_in"], pw["q_norm"],
            pw["w_qa"], pw["w_qb"], pw["kv_norm"]]
    out_shape = [jax.ShapeDtypeStruct((t, KV_RANK), _F32),
                 jax.ShapeDtypeStruct((t, QK_ROPE_DIM), _F32),
                 jax.ShapeDtypeStruct((t, CONV_CH), _F32)]
    out_specs = [pl.BlockSpec((tm, KV_RANK), row), pl.BlockSpec((tm, QK_ROPE_DIM), row),
                 pl.BlockSpec((tm, CONV_CH), row)]
    if absorbed:
        extra = ["w_abs", "w_qr"]
        widths = [N_HEADS * KV_RANK, N_HEADS * QK_ROPE_DIM]
    else:
        extra = ["w_k1", "w_k2", "w_vt"]
        widths = [N_HEADS * HEAD_SLOT, N_HEADS * HEAD_SLOT]
    for name in extra:
        in_specs.append(_const_spec(pw[name].shape))
        args.append(pw[name])
    for w in widths:
        out_shape.append(jax.ShapeDtypeStruct((t, w), _BF16))
        out_specs.append(pl.BlockSpec((tm, w), row))
    if not absorbed:
        out_shape.append(jax.ShapeDtypeStruct((t // tkv, ATTN_WIDTH, tkv), _BF16))
        out_specs.append(pl.BlockSpec((tm // tkv, ATTN_WIDTH, tkv), lambda i: (i, 0, 0)))
    return pl.pallas_call(
        functools.partial(_proj_kernel, absorbed=absorbed),
        grid=(t // tm,),
        in_specs=in_specs,
        out_specs=out_specs,
        out_shape=out_shape,
        compiler_params=pltpu.CompilerParams(
            dimension_semantics=("arbitrary",), vmem_limit_bytes=VMEM_LIMIT),
        name="proj_absorbed" if absorbed else "proj",
    )(*args)


def _prompt_attn_kernel(q_ref, k_ref, vt_ref, km_ref, vtm_ref, g_ref, o_ref, m_sc, l_sc, acc_sc, *, tq, tk):
    i = pl.program_id(1)
    n_full = i * (tq // tk)
    key_r = lax.broadcasted_iota(jnp.int32, (tk, tq), 0)
    qry_c = lax.broadcasted_iota(jnp.int32, (tk, tq), 1)

    def q_slot(h):
        return slice(h * HEAD_SLOT, (h + 1) * HEAD_SLOT)

    def v_rows(h):
        return slice(h * V_HEAD_DIM, (h + 1) * V_HEAD_DIM)

    def update(h, s, vt_t, mask=None, first=False):
        rs = v_rows(h)
        if mask is not None:
            s = jnp.where(mask, s, -jnp.inf)
        s_max = jnp.max(s, axis=0, keepdims=True)
        if first:
            m_new = s_max
            p = jnp.exp2(s - m_new)
            l_sc[h] = jnp.sum(p, axis=0, keepdims=True)
            acc_sc[rs, :] = _dot(vt_t, p.astype(_BF16))
        else:
            m_old = m_sc[h]
            m_new = jnp.maximum(m_old, s_max)
            alpha = jnp.exp2(m_old - m_new)
            p = jnp.exp2(s - m_new)
            l_sc[h] = l_sc[h] * alpha + jnp.sum(p, axis=0, keepdims=True)
            acc_sc[rs, :] = acc_sc[rs, :] * alpha + _dot(vt_t, p.astype(_BF16))
        m_sc[h] = m_new

    def sweep(scores, values, mask=None, first=False):
        pending = [scores(h) for h in range(HEAD_LOOKAHEAD)]
        for h in range(N_HEADS):
            s = pending.pop(0)
            if h + HEAD_LOOKAHEAD < N_HEADS:
                pending.append(scores(h + HEAD_LOOKAHEAD))
            update(h, s, values(h), mask, first)

    def tile(j, mask=None):
        ks = pl.ds(pl.multiple_of(j * tk, tk), tk)
        sweep(lambda h: _dot_nt(k_ref[ks, q_slot(h)], q_ref[:, q_slot(h)]),
              lambda h: vt_ref[j, v_rows(h), :], mask)

    sweep(lambda h: _dot_nt(km_ref[:, q_slot(h)], q_ref[:, q_slot(h)]),
          lambda h: vtm_ref[v_rows(h), :], first=True)

    def full_tile(j, carry):
        tile(j)
        return carry

    lax.fori_loop(0, n_full, full_tile, 0)
    for d in range(tq // tk):
        tile(n_full + d, mask=key_r + d * tk <= qry_c)

    inv_l = 1.0 / l_sc[...]
    a_t = jnp.concatenate(
        [acc_sc[h * V_HEAD_DIM:(h + 1) * V_HEAD_DIM, :] * inv_l[h] for h in range(N_HEADS)], axis=0)
    ms = jnp.mean(a_t * a_t, axis=0, keepdims=True)
    a_n = a_t * lax.rsqrt(ms + EPS) * g_ref[...]
    o_ref[...] = a_n.T.astype(_BF16)


def _prompt_attention(q, k, vt, k_meta, vt_meta, grp_g_col, *, batch, seq, tq, tk):
    nq = seq // tq
    nk = seq // tk
    kw = N_HEADS * HEAD_SLOT
    return pl.pallas_call(
        functools.partial(_prompt_attn_kernel, tq=tq, tk=tk),
        grid=(batch, nq),
        in_specs=[
            pl.BlockSpec((tq, kw), lambda b, i: (b * nq + i, 0)),
            pl.BlockSpec((seq, kw), lambda b, i: (b, 0)),
            pl.BlockSpec((nk, ATTN_WIDTH, tk), lambda b, i: (b, 0, 0)),
            _const_spec(k_meta.shape),
            _const_spec(vt_meta.shape),
            _const_spec((ATTN_WIDTH, 1)),
        ],
        out_specs=pl.BlockSpec((tq, ATTN_WIDTH), lambda b, i: (b * nq + i, 0)),
        out_shape=jax.ShapeDtypeStruct((batch * seq, ATTN_WIDTH), _BF16),
        scratch_shapes=[
            pltpu.VMEM((N_HEADS, 1, tq), _F32),
            pltpu.VMEM((N_HEADS, 1, tq), _F32),
            pltpu.VMEM((ATTN_WIDTH, tq), _F32),
        ],
        compiler_params=pltpu.CompilerParams(
            dimension_semantics=("arbitrary", "arbitrary"), vmem_limit_bytes=VMEM_LIMIT),
        name="prompt_attn",
    )(q, k, vt, k_meta, vt_meta, grp_g_col)


def _sample_attn_kernel(pt_ref, ql_ref, qr_ref, cn_ref, kn_ref, wv_ref, g_ref, ckv_hbm, krt_hbm,
                        o_ref, cbuf, kbuf, sem, m_sc, l_sc, acc_sc, *, pages, chains, n_steps, s_new):
    b = pl.program_id(0)
    j = pl.program_id(1)
    nb = pl.num_programs(0)
    step = b * n_steps + j
    slot = step % 2
    rows = s_new * N_HEADS
    per_step = pages * chains

    def page_copies(bb, jj, sl):
        copies = []
        for p in range(per_step):
            page = pt_ref[bb, jj * per_step + p]
            copies.append(pltpu.make_async_copy(ckv_hbm.at[page], cbuf.at[sl, p], sem.at[0, sl]))
            copies.append(pltpu.make_async_copy(krt_hbm.at[page], kbuf.at[sl, p], sem.at[1, sl]))
        return copies

    @pl.when(step == 0)
    def _():
        for cp in page_copies(0, 0, 0):
            cp.start()

    @pl.when(step + 1 < nb * n_steps)
    def _():
        nxt = step + 1
        for cp in page_copies(nxt // n_steps, nxt % n_steps, 1 - slot):
            cp.start()

    @pl.when(j == 0)
    def _():
        m_sc[...] = jnp.full(m_sc.shape, -jnp.inf, _F32)
        l_sc[...] = jnp.zeros(l_sc.shape, _F32)
        acc_sc[...] = jnp.zeros(acc_sc.shape, _F32)

    for cp in page_copies(b, j, slot):
        cp.wait()

    ql = ql_ref[0]
    qr = qr_ref[0]
    n_keys = pages * PAGE_SIZE
    latents, scores = [], []
    for c in range(chains):
        c_bf = cbuf[slot, c * pages:(c + 1) * pages].reshape(n_keys, KV_RANK).astype(_BF16)
        krt = jnp.concatenate([kbuf[slot, c * pages + p] for p in range(pages)], axis=1).astype(_BF16)
        latents.append(c_bf)
        scores.append(_dot_nt(ql, c_bf) + _dot(qr, krt))
    for c in range(chains):
        c_bf, s = latents[c], scores[c]
        m = m_sc[c]
        m_new = jnp.maximum(m, jnp.max(s, axis=-1, keepdims=True))
        alpha = jnp.exp2(m - m_new)
        p = jnp.exp2(s - m_new)
        l_sc[c] = l_sc[c] * alpha + jnp.sum(p, axis=-1, keepdims=True)
        acc_sc[c] = acc_sc[c] * alpha + _dot(p.astype(_BF16), c_bf)
        m_sc[c] = m_new

    @pl.when(j == n_steps - 1)
    def _():
        qlf = ql.astype(_F32)
        qrf = qr.astype(_F32)
        tok = lax.broadcasted_iota(jnp.int32, (rows, 1), 0) // N_HEADS
        cn = cn_ref[0].astype(_BF16).astype(_F32)
        kn = kn_ref[0].astype(_BF16).astype(_F32)
        s_cols = []
        for t in range(s_new):
            st = (jnp.sum(qlf * cn[t:t + 1, :], axis=-1, keepdims=True)
                  + jnp.sum(qrf * kn[t:t + 1, :], axis=-1, keepdims=True))
            s_cols.append(jnp.where(tok >= t, st, -jnp.inf))
        m_new = m_sc[0]
        for c in range(1, chains):
            m_new = jnp.maximum(m_new, m_sc[c])
        for st in s_cols:
            m_new = jnp.maximum(m_new, st)
        l = jnp.zeros((rows, 1), _F32)
        acc = jnp.zeros((rows, KV_RANK), _F32)
        for c in range(chains):
            alpha = jnp.exp2(m_sc[c] - m_new)
            l = l + l_sc[c] * alpha
            acc = acc + acc_sc[c] * alpha
        for t in range(s_new):
            pt = jnp.exp2(s_cols[t] - m_new)
            l = l + pt
            acc = acc + pt.astype(_BF16).astype(_F32) * cn[t:t + 1, :]
        o = (acc * (1.0 / l)).astype(_BF16)
        full = _dot(o, wv_ref[...])
        head = lax.broadcasted_iota(jnp.int32, (rows, ATTN_WIDTH), 0) % N_HEADS
        lane_head = lax.broadcasted_iota(jnp.int32, (rows, ATTN_WIDTH), 1) // V_HEAD_DIM
        full = jnp.where(head == lane_head, full, 0.0)
        a = jnp.concatenate(
            [jnp.sum(full[t * N_HEADS:(t + 1) * N_HEADS], axis=0, keepdims=True) for t in range(s_new)],
            axis=0)
        o_ref[0] = _rms(a, g_ref[...]).astype(_BF16)


def _sample_attention(page_table, ql, qr, c_new, k_new, w_v, grp_g, cache_ckv, cache_krt, *, pages, chains):
    nb, n_pages = page_table.shape
    per_step = pages * chains
    n_steps = n_pages // per_step
    s_new = c_new.shape[1]
    rows = s_new * N_HEADS
    grid_spec = pltpu.PrefetchScalarGridSpec(
        num_scalar_prefetch=1,
        grid=(nb, n_steps),
        in_specs=[
            pl.BlockSpec((1, rows, KV_RANK), lambda b, j, pt: (b, 0, 0)),
            pl.BlockSpec((1, rows, QK_ROPE_DIM), lambda b, j, pt: (b, 0, 0)),
            pl.BlockSpec((1, s_new, KV_RANK), lambda b, j, pt: (b, 0, 0)),
            pl.BlockSpec((1, s_new, QK_ROPE_DIM), lambda b, j, pt: (b, 0, 0)),
            _const_spec((KV_RANK, ATTN_WIDTH)),
            _const_spec((1, ATTN_WIDTH)),
            pl.BlockSpec(memory_space=pl.ANY),
            pl.BlockSpec(memory_space=pl.ANY),
        ],
        out_specs=pl.BlockSpec((1, s_new, ATTN_WIDTH), lambda b, j, pt: (b, 0, 0)),
        scratch_shapes=[
            pltpu.VMEM((2, per_step, PAGE_SIZE, KV_RANK), _F32),
            pltpu.VMEM((2, per_step, QK_ROPE_DIM, PAGE_SIZE), _F32),
            pltpu.SemaphoreType.DMA((2, 2)),
            pltpu.VMEM((chains, rows, 1), _F32),
            pltpu.VMEM((chains, rows, 1), _F32),
            pltpu.VMEM((chains, rows, KV_RANK), _F32),
        ],
    )
    return pl.pallas_call(
        functools.partial(_sample_attn_kernel, pages=pages, chains=chains, n_steps=n_steps, s_new=s_new),
        grid_spec=grid_spec,
        out_shape=jax.ShapeDtypeStruct((nb, s_new, ATTN_WIDTH), _BF16),
        compiler_params=pltpu.CompilerParams(
            dimension_semantics=("arbitrary", "arbitrary"), vmem_limit_bytes=VMEM_LIMIT),
        name="sample_attn",
    )(page_table, ql, qr, c_new, k_new, w_v, grp_g, cache_ckv, cache_krt)


def _mix_tail(y, h, a_n, cb_ref, lng_ref, lnb_ref, cg_ref, woa_ref, woc_ref):
    y = y + cb_ref[...]
    mu = jnp.mean(y, axis=-1, keepdims=True)
    yc = y - mu
    var = jnp.mean(yc * yc, axis=-1, keepdims=True)
    y = yc * lax.rsqrt(var + EPS) * lng_ref[...] + lnb_ref[...]
    y = y * _sigmoid(y)
    cv = _rms(y, cg_ref[...]).astype(_BF16)
    return h + _dot(a_n, woa_ref[...]) + _dot(cv, woc_ref[...])


def _mix_prompt_kernel(h_ref, a_ref, glu_ref, prev_ref, first_ref, cw_ref, cb_ref, lng_ref, lnb_ref,
                       cg_ref, woa_ref, woc_ref, o_ref, ext_sc, y_sc, *, tm):
    i = pl.program_id(1)
    ext_sc[:HALO, :] = jnp.where(i == 0, first_ref[...], prev_ref[...])
    ext_sc[HALO:, :] = glu_ref[...]
    off = HALO - (CONV_WIDTH - 1)
    for r0 in range(0, tm, CONV_ROWS):
        for c0 in range(0, CONV_CH, LANES):
            cs = slice(c0, c0 + LANES)
            y = jnp.zeros((CONV_ROWS, LANES), _F32)
            for res in range(SUBLANES):
                taps = [k for k in range(CONV_WIDTH) if (off + k) % SUBLANES == res]
                n_rows = CONV_ROWS + (SUBLANES if res else 0)
                z = jnp.zeros((n_rows, LANES), _F32)
                for k in taps:
                    base = r0 + off + k - res
                    z = z + ext_sc[base:base + n_rows, cs] * cw_ref[k:k + 1, cs]
                y = y + z[res:res + CONV_ROWS]
            y_sc[r0:r0 + CONV_ROWS, cs] = y
    o_ref[...] = _mix_tail(y_sc[...], h_ref[...], a_ref[...], cb_ref, lng_ref, lnb_ref, cg_ref,
                           woa_ref, woc_ref)


def _mix_prompt(h, a_n, glu, first_halo, mw, *, batch, seq, tm):
    nt = seq // tm
    d = h.shape[1]
    hb = tm // HALO
    row = lambda b, i: (b * nt + i, 0)
    return pl.pallas_call(
        functools.partial(_mix_prompt_kernel, tm=tm),
        grid=(batch, nt),
        in_specs=[
            pl.BlockSpec((tm, d), row),
            pl.BlockSpec((tm, ATTN_WIDTH), row),
            pl.BlockSpec((tm, CONV_CH), row),
            pl.BlockSpec((HALO, CONV_CH), lambda b, i: (jnp.maximum((b * nt + i) * hb - 1, 0), 0)),
            _const_spec((HALO, CONV_CH)),
            _const_spec(mw["conv_w"].shape), _const_spec(mw["conv_b"].shape),
            _const_spec(mw["ln_g"].shape), _const_spec(mw["ln_b"].shape),
            _const_spec(mw["conv_grp"].shape),
            _const_spec(mw["w_out_a"].shape), _const_spec(mw["w_out_c"].shape),
        ],
        out_specs=pl.BlockSpec((tm, d), row),
        out_shape=jax.ShapeDtypeStruct(h.shape, _F32),
        scratch_shapes=[pltpu.VMEM((HALO + tm, CONV_CH), _F32),
                        pltpu.VMEM((tm, CONV_CH), _F32)],
        compiler_params=pltpu.CompilerParams(
            dimension_semantics=("arbitrary", "arbitrary"), vmem_limit_bytes=VMEM_LIMIT),
        name="mix_prompt",
    )(h, a_n, glu, glu, first_halo, mw["conv_w"], mw["conv_b"], mw["ln_g"], mw["ln_b"],
      mw["conv_grp"], mw["w_out_a"], mw["w_out_c"])


def _mix_sample_kernel(h_ref, a_ref, full_ref, cw_ref, cb_ref, lng_ref, lnb_ref,
                       cg_ref, woa_ref, woc_ref, o_ref, *, s_new):
    ys = []
    for t in range(s_new):
        y = jnp.zeros(full_ref.shape[1:], _F32)
        for k in range(CONV_WIDTH):
            y = y + full_ref[t + k] * cw_ref[k:k + 1, :]
        ys.append(y)
    y = jnp.concatenate(ys, axis=0)
    o_ref[...] = _mix_tail(y, h_ref[...], a_ref[...], cb_ref, lng_ref, lnb_ref, cg_ref, woa_ref, woc_ref)


def _mix_sample(h, a_n, full_t, mw, *, s_new):
    t, d = h.shape
    return pl.pallas_call(
        functools.partial(_mix_sample_kernel, s_new=s_new),
        grid=(1,),
        in_specs=[
            _const_spec((t, d)), _const_spec((t, ATTN_WIDTH)), _const_spec(full_t.shape),
            _const_spec(mw["conv_w"].shape), _const_spec(mw["conv_b"].shape),
            _const_spec(mw["ln_g"].shape), _const_spec(mw["ln_b"].shape),
            _const_spec(mw["conv_grp"].shape),
            _const_spec(mw["w_out_a"].shape), _const_spec(mw["w_out_c"].shape),
        ],
        out_specs=pl.BlockSpec((t, d), lambda i: (0, 0)),
        out_shape=jax.ShapeDtypeStruct((t, d), _F32),
        compiler_params=pltpu.CompilerParams(
            dimension_semantics=("arbitrary",), vmem_limit_bytes=VMEM_LIMIT),
        name="mix_sample",
    )(h, a_n, full_t, mw["conv_w"], mw["conv_b"], mw["ln_g"], mw["ln_b"],
      mw["conv_grp"], mw["w_out_a"], mw["w_out_c"])


def _rope_tables(pos):
    half = QK_ROPE_DIM // 2
    inv = ROPE_THETA ** (-jnp.arange(half, dtype=_F32) / half)
    ang = pos.astype(_F32)[:, None] * inv[None, :]
    cos = jnp.cos(ang)
    sin = jnp.sin(ang)
    n = pos.shape[0]
    cos2 = jnp.concatenate([cos, cos], axis=-1)
    sin2 = jnp.concatenate([sin, sin], axis=-1)
    pad_q = jnp.zeros((n, HEAD_SLOT - Q_HEAD_DIM), _F32)
    cosq = SCORE_SCALE * jnp.concatenate([jnp.ones((n, QK_NOPE_DIM), _F32), cos2, pad_q], axis=-1)
    sinq = SCORE_SCALE * jnp.concatenate([jnp.zeros((n, QK_NOPE_DIM), _F32), sin2, pad_q], axis=-1)
    pad_k = jnp.zeros((n, LANES - QK_ROPE_DIM), _F32)
    cosk = jnp.concatenate([cos2, pad_k], axis=-1)
    sink = jnp.concatenate([sin2, pad_k], axis=-1)
    return cosq, sinq, cosk, sink


def _rot_cols(w):
    half = QK_ROPE_DIM // 2
    return jnp.concatenate([-w[..., half:], w[..., :half]], axis=-1)


def _layer_weights(l, mix_norm, w_in, q_norm, w_uq, kv_norm, w_uk, w_uv):
    d = w_in.shape[1]
    win = w_in[l]
    o_kv = Q_RANK
    o_kr = o_kv + KV_RANK
    o_cv = o_kr + QK_ROPE_DIM
    w_kr = win[:, o_kr:o_cv]
    zpad = jnp.zeros((d, LANES - QK_ROPE_DIM), _F32)
    win_p = jnp.concatenate([win[:, :o_kr], win[:, o_cv:], w_kr, zpad, _rot_cols(w_kr), zpad], axis=-1)

    wq = w_uq[l].reshape(Q_RANK, N_HEADS, Q_HEAD_DIM)
    zq = jnp.zeros((Q_RANK, N_HEADS, HEAD_SLOT - Q_HEAD_DIM), _F32)
    w_qa = jnp.concatenate([wq, zq], axis=-1).reshape(Q_RANK, N_HEADS * HEAD_SLOT)
    w_qb = jnp.concatenate([jnp.zeros((Q_RANK, N_HEADS, QK_NOPE_DIM), _F32),
                            _rot_cols(wq[..., QK_NOPE_DIM:]), zq], axis=-1).reshape(Q_RANK, N_HEADS * HEAD_SLOT)

    wuk = w_uk[l]
    zk = jnp.zeros((KV_RANK, N_HEADS, HEAD_SLOT - QK_NOPE_DIM), _F32)
    w_k1 = jnp.concatenate([wuk, zk], axis=-1).reshape(KV_RANK, N_HEADS * HEAD_SLOT)
    eye = jnp.eye(QK_ROPE_DIM, dtype=_F32)
    place = jnp.concatenate([jnp.zeros((QK_ROPE_DIM, QK_NOPE_DIM), _F32), eye,
                             jnp.zeros((QK_ROPE_DIM, HEAD_SLOT - Q_HEAD_DIM), _F32)], axis=-1)
    w_k2 = jnp.concatenate([jnp.tile(place, (1, N_HEADS)),
                            jnp.zeros((LANES - QK_ROPE_DIM, N_HEADS * HEAD_SLOT), _F32)], axis=0)
    w_v = w_uv[l].reshape(KV_RANK, ATTN_WIDTH)

    w_abs = jnp.zeros((N_HEADS, HEAD_SLOT, N_HEADS, KV_RANK), _F32)
    w_qr = jnp.zeros((N_HEADS, HEAD_SLOT, N_HEADS, QK_ROPE_DIM), _F32)
    for h in range(N_HEADS):
        w_abs = w_abs.at[h, :QK_NOPE_DIM, h, :].set(wuk[:, h, :].T)
        w_qr = w_qr.at[h, QK_NOPE_DIM:Q_HEAD_DIM, h, :].set(eye)
    w_abs = w_abs.reshape(N_HEADS * HEAD_SLOT, N_HEADS * KV_RANK)
    w_qr = w_qr.reshape(N_HEADS * HEAD_SLOT, N_HEADS * QK_ROPE_DIM)

    bf = lambda x: x.astype(_BF16)
    return {
        "mix_norm": mix_norm[l][None], "w_in": bf(win_p), "q_norm": q_norm[l][None],
        "w_qa": bf(w_qa), "w_qb": bf(w_qb), "kv_norm": kv_norm[l][None],
        "w_k1": bf(w_k1), "w_k2": bf(w_k2), "w_v": bf(w_v), "w_vt": bf(w_v.T),
        "w_abs": bf(w_abs), "w_qr": bf(w_qr),
    }


def _pick_tile(n, prefer):
    for t in prefer:
        if n % t == 0:
            return t
    return n


def kernel(x_prompt, x_sample, cache_kv_latent, cache_k_rope, state_conv, page_table, meta_tokens,
           ffn1_norm, ffn1_w_gate, ffn1_w_up, ffn1_w_down, mix_norm, w_in, q_norm, w_uq, kv_norm,
           w_uk, w_uv, conv_w, conv_b, conv_ln_g, conv_ln_b, attn_grp_norm, conv_grp_norm, w_out,
           ffn2_norm, ffn2_w_gate, ffn2_w_up, ffn2_w_down, final_norm):
    b_p, s_p, d = x_prompt.shape
    b_s, s_s, _ = x_sample.shape
    depth = ffn1_norm.shape[0]
    assert depth == 1, "the meta rows' later-layer inputs are not carried; single layer only"
    n_pages = page_table.shape[1]
    past_len = n_pages * PAGE_SIZE
    assert CONV_WIDTH - 1 <= HALO and N_META <= HALO
    l = 0
    bf = lambda x: x.astype(_BF16)

    pw = _layer_weights(l, mix_norm, w_in, q_norm, w_uq, kv_norm, w_uk, w_uv)
    f1 = (ffn1_norm[l][None], bf(ffn1_w_gate[l]), bf(ffn1_w_up[l]), bf(ffn1_w_down[l]))
    f2 = (ffn2_norm[l][None], bf(ffn2_w_gate[l]), bf(ffn2_w_up[l]), bf(ffn2_w_down[l]))
    fin = final_norm[None]
    mw = {
        "conv_w": jnp.concatenate([conv_w[l], jnp.zeros((1, CONV_CH), _F32)], axis=0),
        "conv_b": conv_b[l][None], "ln_g": conv_ln_g[l][None], "ln_b": conv_ln_b[l][None],
        "conv_grp": conv_grp_norm[l][None],
        "w_out_a": bf(w_out[l][:ATTN_WIDTH]), "w_out_c": bf(w_out[l][ATTN_WIDTH:]),
    }

    tm_p = _pick_tile(s_p, (512, 256, 128))
    tq = _pick_tile(s_p, (512, 256, 128))
    tk = _pick_tile(tq, (256, 128))
    assert tm_p % CONV_ROWS == 0 and tm_p % HALO == 0

    h_m = _ffn(meta_tokens, *f1, fin, tm=N_META, final=False)
    tab_m = _rope_tables(jnp.arange(N_META, dtype=jnp.int32))
    ckv_m, kr_m, glu_m, _, k_m, vt_m = _proj(h_m, tab_m, 1, pw, tm=N_META, absorbed=False, tkv=N_META)
    first_halo = jnp.concatenate([jnp.zeros((HALO - N_META, CONV_CH), _F32), glu_m], axis=0)

    xp = x_prompt.reshape(b_p * s_p, d)
    h_p = _ffn(xp, *f1, fin, tm=tm_p, final=False)
    tab_p = _rope_tables(N_META + jnp.arange(s_p, dtype=jnp.int32))
    ckv_p, kr_p, glu_p, q_p, k_p, vt_p = _proj(h_p, tab_p, s_p // tm_p, pw, tm=tm_p, absorbed=False, tkv=tk)
    a_p = _prompt_attention(q_p, k_p, vt_p, k_m, vt_m[0], attn_grp_norm[l][:, None],
                            batch=b_p, seq=s_p, tq=tq, tk=tk)
    h2_p = _mix_prompt(h_p, a_p, glu_p, first_halo, mw, batch=b_p, seq=s_p, tm=tm_p)
    y_p = _ffn(h2_p, *f2, fin, tm=tm_p, final=True).reshape(b_p, s_p, d)

    n_s = b_s * s_s
    xs = x_sample.transpose(1, 0, 2).reshape(n_s, d)
    h_s = _ffn(xs, *f1, fin, tm=n_s, final=False)
    pos_s = past_len + jnp.repeat(jnp.arange(s_s, dtype=jnp.int32), b_s)
    ckv_s, kr_s, glu_s, ql_s, qr_s = _proj(h_s, _rope_tables(pos_s), 1, pw, tm=n_s, absorbed=True)
    seq_major = lambda x: x.reshape(s_s, b_s, -1).transpose(1, 0, 2)
    ckv_sb, kr_sb, glu_sb = seq_major(ckv_s), seq_major(kr_s), seq_major(glu_s)
    ql_b = seq_major(ql_s).reshape(b_s, s_s * N_HEADS, KV_RANK)
    qr_b = seq_major(qr_s).reshape(b_s, s_s * N_HEADS, QK_ROPE_DIM)
    chains = 2 if n_pages % 2 == 0 else 1
    a_s = _sample_attention(page_table, ql_b, qr_b, ckv_sb, kr_sb, pw["w_v"], attn_grp_norm[l][None],
                            cache_kv_latent[l], jnp.swapaxes(cache_k_rope[l], 1, 2),
                            pages=_pick_tile(n_pages // chains, (16, 8, 4, 2)), chains=chains)
    a_st = a_s.transpose(1, 0, 2).reshape(n_s, ATTN_WIDTH)
    full_t = jnp.concatenate([state_conv[l].transpose(1, 0, 2), glu_s.reshape(s_s, b_s, CONV_CH)], axis=0)
    h2_s = _mix_sample(h_s, a_st, full_t, mw, s_new=s_s)
    y_s = _ffn(h2_s, *f2, fin, tm=n_s, final=True).reshape(s_s, b_s, d).transpose(1, 0, 2)

    bcast = lambda x: jnp.broadcast_to(x[None], (b_p,) + x.shape)
    new_ckv_p = jnp.concatenate([bcast(ckv_m), ckv_p.reshape(b_p, s_p, KV_RANK)], axis=1)[None]
    new_kr_p = jnp.concatenate([bcast(kr_m), kr_p.reshape(b_p, s_p, QK_ROPE_DIM)], axis=1)[None]
    n_keep = CONV_WIDTH - 1
    new_conv_p = glu_p.reshape(b_p, s_p, CONV_CH)[:, s_p - n_keep:][None]
    new_conv_s = jnp.concatenate([state_conv[l], glu_sb], axis=1)[:, -n_keep:][None]
    return (y_p, y_s, new_ckv_p, new_kr_p, new_conv_p, ckv_sb[None], kr_sb[None], new_conv_s)
```

```python
import functools
import math

import jax
import jax.numpy as jnp
from jax import lax
from jax.experimental import pallas as pl
from jax.experimental.pallas import tpu as pltpu

N_META = 16
N_HEADS = 8
QK_NOPE_DIM = 64
QK_ROPE_DIM = 32
V_HEAD_DIM = 64
Q_RANK = 256
KV_RANK = 256
CONV_CH = 512
CONV_WIDTH = 31
ROPE_THETA = 10000.0
EPS = 1e-6
PAGE_SIZE = 128
Q_HEAD_DIM = QK_NOPE_DIM + QK_ROPE_DIM
ATTN_WIDTH = N_HEADS * V_HEAD_DIM
SOFTMAX_SCALE = Q_HEAD_DIM ** -0.5
SCORE_SCALE = SOFTMAX_SCALE * math.log2(math.e)

LANES = 128
SUBLANES = 8
HEAD_SLOT = LANES
HALO = 32
CONV_ROWS = 128
HEAD_LOOKAHEAD = 3
VT_ONES = 16
VT_ROWS = V_HEAD_DIM + VT_ONES
VMEM_LIMIT = 56 * 1024 * 1024

_F32 = jnp.float32
_BF16 = jnp.bfloat16


def _const_spec(shape):
    nd = len(shape)
    return pl.BlockSpec(shape, lambda *_: (0,) * nd, pipeline_mode=pl.Buffered(1))


def _rms(x, g):
    return x * lax.rsqrt(jnp.mean(x * x, axis=-1, keepdims=True) + EPS) * g


def _sigmoid(x):
    return 1.0 / (1.0 + jnp.exp(-x))


def _dot(a, b):
    return jnp.dot(a, b, preferred_element_type=_F32)


def _dot_nt(a, b):
    return lax.dot_general(a, b, (((1,), (1,)), ((), ())), preferred_element_type=_F32)


def _ffn_kernel(x_ref, g_ref, wg_ref, wu_ref, wd_ref, fg_ref, o_ref, *, ff_chunk, final):
    x = x_ref[...]
    xn = _rms(x, g_ref[...]).astype(_BF16)
    d_ff = wg_ref.shape[1]
    acc = jnp.zeros_like(x)
    for c0 in range(0, d_ff, ff_chunk):
        gate = _dot(xn, wg_ref[:, c0:c0 + ff_chunk])
        up = _dot(xn, wu_ref[:, c0:c0 + ff_chunk])
        act = (gate * _sigmoid(gate) * up).astype(_BF16)
        acc = acc + _dot(act, wd_ref[c0:c0 + ff_chunk, :])
    y = x + 0.5 * acc
    if final:
        y = _rms(y, fg_ref[...])
    o_ref[...] = y


def _ffn(x, norm_g, wg, wu, wd, final_g, *, tm, final):
    t, d = x.shape
    d_ff = wg.shape[1]
    ff_chunk = d_ff // 2 if (d_ff // 2) % LANES == 0 else d_ff
    return pl.pallas_call(
        functools.partial(_ffn_kernel, ff_chunk=ff_chunk, final=final),
        grid=(t // tm,),
        in_specs=[
            pl.BlockSpec((tm, d), lambda i: (i, 0)),
            _const_spec((1, d)),
            _const_spec((d, d_ff)),
            _const_spec((d, d_ff)),
            _const_spec((d_ff, d)),
            _const_spec((1, d)),
        ],
        out_specs=pl.BlockSpec((tm, d), lambda i: (i, 0)),
        out_shape=jax.ShapeDtypeStruct((t, d), _F32),
        compiler_params=pltpu.CompilerParams(
            dimension_semantics=("arbitrary",), vmem_limit_bytes=VMEM_LIMIT),
        name="ffn_final" if final else "ffn",
    )(x, norm_g, wg, wu, wd, final_g)


def _proj_kernel(h_ref, cosq_ref, sinq_ref, cosk_ref, sink_ref,
                 mixg_ref, win_ref, qg_ref, wqa_ref, wqb_ref, kvg_ref, *rest, absorbed):
    if absorbed:
        wabs_ref, wqr_ref, ckv_ref, kr_ref, glu_ref, ql_ref, qr_ref = rest
    else:
        wk1_ref, wk2_ref, wvt_ref, vtb_ref, ckv_ref, kr_ref, glu_ref, q_ref, k_ref, vt_ref = rest
    n = _rms(h_ref[...], mixg_ref[...]).astype(_BF16)
    z = _dot(n, win_ref[...])
    o_kv = Q_RANK
    o_ca = o_kv + KV_RANK
    o_cb = o_ca + CONV_CH
    o_ka = o_cb + CONV_CH
    o_kb = o_ka + LANES

    glu_ref[...] = z[:, o_ca:o_cb] * _sigmoid(z[:, o_cb:o_ka])

    c = _rms(z[:, o_kv:o_ca], kvg_ref[...])
    ckv_ref[...] = c
    c_bf = c.astype(_BF16)
    kr = z[:, o_ka:o_kb] * cosk_ref[...] + z[:, o_kb:o_kb + LANES] * sink_ref[...]
    kr_ref[...] = kr[:, :QK_ROPE_DIM]

    qn = _rms(z[:, :Q_RANK], qg_ref[...]).astype(_BF16)
    qa = _dot(qn, wqa_ref[...])
    qb = _dot(qn, wqb_ref[...])
    cosq = cosq_ref[...]
    sinq = sinq_ref[...]
    q_heads = []
    for h in range(N_HEADS):
        sl = slice(h * HEAD_SLOT, (h + 1) * HEAD_SLOT)
        q_heads.append((qa[:, sl] * cosq + qb[:, sl] * sinq).astype(_BF16))
    q = jnp.concatenate(q_heads, axis=-1)
    if absorbed:
        ql_ref[...] = _dot(q, wabs_ref[...]).astype(_BF16)
        qr_ref[...] = _dot(q, wqr_ref[...]).astype(_BF16)
    else:
        q_ref[...] = q
        k_ref[...] = (_dot(c_bf, wk1_ref[...]) + _dot(kr.astype(_BF16), wk2_ref[...])).astype(_BF16)
        vt = (_dot_nt(wvt_ref[...], c_bf) + vtb_ref[...]).astype(_BF16)
        tkv = vt_ref.shape[2]
        for blk in range(vt_ref.shape[0]):
            vt_ref[blk] = vt[:, blk * tkv:(blk + 1) * tkv]


def _proj(h, tables, n_table_blocks, pw, *, tm, absorbed, tkv=None):
    t, d = h.shape
    cosq, sinq, cosk, sink = tables
    row = lambda i: (i, 0)
    tab = lambda i: (i % n_table_blocks, 0)
    in_specs = [
        pl.BlockSpec((tm, d), row),
        pl.BlockSpec((tm, LANES), tab), pl.BlockSpec((tm, LANES), tab),
        pl.BlockSpec((tm, LANES), tab), pl.BlockSpec((tm, LANES), tab),
        _const_spec(pw["mix_norm"].shape), _const_spec(pw["w_in"].shape),
        _const_spec(pw["q_norm"].shape), _const_spec(pw["w_qa"].shape),
        _const_spec(pw["w_qb"].shape), _const_spec(pw["kv_norm"].shape),
    ]
    args = [h, cosq, sinq, cosk, sink, pw["mix_norm"], pw["w_in"], pw["q_norm"],
            pw["w_qa"], pw["w_qb"], pw["kv_norm"]]
    out_shape = [jax.ShapeDtypeStruct((t, KV_RANK), _F32),
                 jax.ShapeDtypeStruct((t, QK_ROPE_DIM), _F32),
                 jax.ShapeDtypeStruct((t, CONV_CH), _F32)]
    out_specs = [pl.BlockSpec((tm, KV_RANK), row), pl.BlockSpec((tm, QK_ROPE_DIM), row),
                 pl.BlockSpec((tm, CONV_CH), row)]
    if absorbed:
        extra = ["w_abs", "w_qr"]
        widths = [N_HEADS * KV_RANK, N_HEADS * QK_ROPE_DIM]
    else:
        extra = ["w_k1", "w_k2", "w_vt", "vt_bias"]
        widths = [N_HEADS * HEAD_SLOT, N_HEADS * HEAD_SLOT]
    for name in extra:
        in_specs.append(_const_spec(pw[name].shape))
        args.append(pw[name])
    for w in widths:
        out_shape.append(jax.ShapeDtypeStruct((t, w), _BF16))
        out_specs.append(pl.BlockSpec((tm, w), row))
    if not absorbed:
        vr = N_HEADS * VT_ROWS
        out_shape.append(jax.ShapeDtypeStruct((t // tkv, vr, tkv), _BF16))
        out_specs.append(pl.BlockSpec((tm // tkv, vr, tkv), lambda i: (i, 0, 0)))
    return pl.pallas_call(
        functools.partial(_proj_kernel, absorbed=absorbed),
        grid=(t // tm,),
        in_specs=in_specs,
        out_specs=out_specs,
        out_shape=out_shape,
        compiler_params=pltpu.CompilerParams(
            dimension_semantics=("arbitrary",), vmem_limit_bytes=VMEM_LIMIT),
        name="proj_absorbed" if absorbed else "proj",
    )(*args)


def _prompt_attn_kernel(q_ref, k_ref, vt_ref, km_ref, vtm_ref, g_ref, o_ref, m_sc, acc_sc, *, tq, tk):
    i = pl.program_id(1)
    n_full = i * (tq // tk)

    def q_slot(h):
        return slice(h * HEAD_SLOT, (h + 1) * HEAD_SLOT)

    def v_rows(h):
        return slice(h * VT_ROWS, (h + 1) * VT_ROWS)

    def update(h, s, vt_t, c0, mask=None, first=False):
        rs = v_rows(h)
        if mask is not None:
            s = jnp.where(mask, s, -jnp.inf)
        s_max = jnp.max(s, axis=0, keepdims=True)
        if first:
            m_new = s_max
            acc_sc[rs, c0:] = _dot(vt_t, jnp.exp2(s - m_new).astype(_BF16))
        else:
            m_old = m_sc[h, :, c0:]
            m_new = jnp.maximum(m_old, s_max)
            alpha = jnp.exp2(m_old - m_new)
            p = jnp.exp2(s - m_new)
            acc_sc[rs, c0:] = acc_sc[rs, c0:] * alpha + _dot(vt_t, p.astype(_BF16))
        m_sc[h, :, c0:] = m_new

    def sweep(scores, values, c0=0, mask=None, first=False):
        pending = [scores(h) for h in range(HEAD_LOOKAHEAD)]
        for h in range(N_HEADS):
            s = pending.pop(0)
            if h + HEAD_LOOKAHEAD < N_HEADS:
                pending.append(scores(h + HEAD_LOOKAHEAD))
            update(h, s, values(h), c0, mask, first)

    def tile(j, c0=0, mask=None):
        ks = pl.ds(pl.multiple_of(j * tk, tk), tk)
        sweep(lambda h: _dot_nt(k_ref[ks, q_slot(h)], q_ref[c0:, q_slot(h)]),
              lambda h: vt_ref[j, v_rows(h), :], c0, mask)

    sweep(lambda h: _dot_nt(km_ref[:, q_slot(h)], q_ref[:, q_slot(h)]),
          lambda h: vtm_ref[v_rows(h), :], first=True)

    def full_tile(j, carry):
        tile(j)
        return carry

    lax.fori_loop(0, n_full, full_tile, 0)
    for d in range(tq // tk):
        c0 = d * tk
        key_r = lax.broadcasted_iota(jnp.int32, (tk, tq - c0), 0)
        qry_c = lax.broadcasted_iota(jnp.int32, (tk, tq - c0), 1)
        tile(n_full + d, c0, mask=key_r <= qry_c)

    heads = []
    for h in range(N_HEADS):
        r0 = h * VT_ROWS
        inv_l = 1.0 / acc_sc[r0 + V_HEAD_DIM:r0 + V_HEAD_DIM + 1, :]
        heads.append(acc_sc[r0:r0 + V_HEAD_DIM, :] * inv_l)
    a_t = jnp.concatenate(heads, axis=0)
    ms = jnp.mean(a_t * a_t, axis=0, keepdims=True)
    a_n = a_t * lax.rsqrt(ms + EPS) * g_ref[...]
    o_ref[...] = a_n.T.astype(_BF16)


def _prompt_attention(q, k, vt, k_meta, vt_meta, grp_g_col, *, batch, seq, tq, tk):
    nq = seq // tq
    nk = seq // tk
    kw = N_HEADS * HEAD_SLOT
    vr = N_HEADS * VT_ROWS
    return pl.pallas_call(
        functools.partial(_prompt_attn_kernel, tq=tq, tk=tk),
        grid=(batch, nq),
        in_specs=[
            pl.BlockSpec((tq, kw), lambda b, i: (b * nq + i, 0)),
            pl.BlockSpec((seq, kw), lambda b, i: (b, 0)),
            pl.BlockSpec((nk, vr, tk), lambda b, i: (b, 0, 0)),
            _const_spec(k_meta.shape),
            _const_spec(vt_meta.shape),
            _const_spec((ATTN_WIDTH, 1)),
        ],
        out_specs=pl.BlockSpec((tq, ATTN_WIDTH), lambda b, i: (b * nq + i, 0)),
        out_shape=jax.ShapeDtypeStruct((batch * seq, ATTN_WIDTH), _BF16),
        scratch_shapes=[
            pltpu.VMEM((N_HEADS, 1, tq), _F32),
            pltpu.VMEM((vr, tq), _F32),
        ],
        compiler_params=pltpu.CompilerParams(
            dimension_semantics=("arbitrary", "arbitrary"), vmem_limit_bytes=VMEM_LIMIT),
        name="prompt_attn",
    )(q, k, vt, k_meta, vt_meta, grp_g_col)


def _sample_attn_kernel(pt_ref, ql_ref, qr_ref, cn_ref, kn_ref, wv_ref, g_ref, ckv_hbm, krt_hbm,
                        o_ref, cbuf, kbuf, sem, m_sc, l_sc, acc_sc, *, pages, chains, n_steps, s_new):
    b = pl.program_id(0)
    j = pl.program_id(1)
    nb = pl.num_programs(0)
    step = b * n_steps + j
    slot = step % 2
    rows = s_new * N_HEADS
    per_step = pages * chains

    def page_copy(page, p, sl):
        return (pltpu.make_async_copy(ckv_hbm.at[page], cbuf.at[sl, p], sem.at[0, sl]),
                pltpu.make_async_copy(krt_hbm.at[page], kbuf.at[sl, p], sem.at[1, sl]))

    def start_pages(bb, jj, sl):
        for p in range(per_step):
            for cp in page_copy(pt_ref[bb, jj * per_step + p], p, sl):
                cp.start(priority=p % 2)

    @pl.when(step == 0)
    def _():
        start_pages(0, 0, 0)

    @pl.when(step + 1 < nb * n_steps)
    def _():
        nxt = step + 1
        start_pages(nxt // n_steps, nxt % n_steps, 1 - slot)

    @pl.when(j == 0)
    def _():
        m_sc[...] = jnp.full(m_sc.shape, -jnp.inf, _F32)
        l_sc[...] = jnp.zeros(l_sc.shape, _F32)
        acc_sc[...] = jnp.zeros(acc_sc.shape, _F32)

    for p in range(per_step):
        for cp in page_copy(0, p, slot):
            cp.wait()

    ql = ql_ref[0]
    qr = qr_ref[0]
    n_keys = pages * PAGE_SIZE
    latents, scores = [], []
    for c in range(chains):
        c_bf = cbuf[slot, c * pages:(c + 1) * pages].reshape(n_keys, KV_RANK).astype(_BF16)
        krt = jnp.concatenate([kbuf[slot, c * pages + p] for p in range(pages)], axis=1).astype(_BF16)
        latents.append(c_bf)
        scores.append(_dot_nt(ql, c_bf) + _dot(qr, krt))
    for c in range(chains):
        c_bf, s = latents[c], scores[c]
        m = m_sc[c]
        m_new = jnp.maximum(m, jnp.max(s, axis=-1, keepdims=True))
        alpha = jnp.exp2(m - m_new)
        p = jnp.exp2(s - m_new)
        l_sc[c] = l_sc[c] * alpha + jnp.sum(p, axis=-1, keepdims=True)
        acc_sc[c] = acc_sc[c] * alpha + _dot(p.astype(_BF16), c_bf)
        m_sc[c] = m_new

    @pl.when(j == n_steps - 1)
    def _():
        qlf = ql.astype(_F32)
        qrf = qr.astype(_F32)
        tok = lax.broadcasted_iota(jnp.int32, (rows, 1), 0) // N_HEADS
        cn = cn_ref[0].astype(_BF16).astype(_F32)
        kn = kn_ref[0].astype(_BF16).astype(_F32)
        s_cols = []
        for t in range(s_new):
            st = (jnp.sum(qlf * cn[t:t + 1, :], axis=-1, keepdims=True)
                  + jnp.sum(qrf * kn[t:t + 1, :], axis=-1, keepdims=True))
            s_cols.append(jnp.where(tok >= t, st, -jnp.inf))
        m_new = m_sc[0]
        for c in range(1, chains):
            m_new = jnp.maximum(m_new, m_sc[c])
        for st in s_cols:
            m_new = jnp.maximum(m_new, st)
        l = jnp.zeros((rows, 1), _F32)
        acc = jnp.zeros((rows, KV_RANK), _F32)
        for c in range(chains):
            alpha = jnp.exp2(m_sc[c] - m_new)
            l = l + l_sc[c] * alpha
            acc = acc + acc_sc[c] * alpha
        for t in range(s_new):
            pt = jnp.exp2(s_cols[t] - m_new)
            l = l + pt
            acc = acc + pt.astype(_BF16).astype(_F32) * cn[t:t + 1, :]
        o = (acc * (1.0 / l)).astype(_BF16)
        full = _dot(o, wv_ref[...])
        head = lax.broadcasted_iota(jnp.int32, (rows, ATTN_WIDTH), 0) % N_HEADS
        lane_head = lax.broadcasted_iota(jnp.int32, (rows, ATTN_WIDTH), 1) // V_HEAD_DIM
        full = jnp.where(head == lane_head, full, 0.0)
        a = jnp.concatenate(
            [jnp.sum(full[t * N_HEADS:(t + 1) * N_HEADS], axis=0, keepdims=True) for t in range(s_new)],
            axis=0)
        o_ref[0] = _rms(a, g_ref[...]).astype(_BF16)


def _sample_attention(page_table, ql, qr, c_new, k_new, w_v, grp_g, cache_ckv, cache_krt, *, pages, chains):
    nb, n_pages = page_table.shape
    per_step = pages * chains
    n_steps = n_pages // per_step
    s_new = c_new.shape[1]
    rows = s_new * N_HEADS
    grid_spec = pltpu.PrefetchScalarGridSpec(
        num_scalar_prefetch=1,
        grid=(nb, n_steps),
        in_specs=[
            pl.BlockSpec((1, rows, KV_RANK), lambda b, j, pt: (b, 0, 0)),
            pl.BlockSpec((1, rows, QK_ROPE_DIM), lambda b, j, pt: (b, 0, 0)),
            pl.BlockSpec((1, s_new, KV_RANK), lambda b, j, pt: (b, 0, 0)),
            pl.BlockSpec((1, s_new, QK_ROPE_DIM), lambda b, j, pt: (b, 0, 0)),
            _const_spec((KV_RANK, ATTN_WIDTH)),
            _const_spec((1, ATTN_WIDTH)),
            pl.BlockSpec(memory_space=pl.ANY),
            pl.BlockSpec(memory_space=pl.ANY),
        ],
        out_specs=pl.BlockSpec((1, s_new, ATTN_WIDTH), lambda b, j, pt: (b, 0, 0)),
        scratch_shapes=[
            pltpu.VMEM((2, per_step, PAGE_SIZE, KV_RANK), _F32),
            pltpu.VMEM((2, per_step, QK_ROPE_DIM, PAGE_SIZE), _F32),
            pltpu.SemaphoreType.DMA((2, 2)),
            pltpu.VMEM((chains, rows, 1), _F32),
            pltpu.VMEM((chains, rows, 1), _F32),
            pltpu.VMEM((chains, rows, KV_RANK), _F32),
        ],
    )
    return pl.pallas_call(
        functools.partial(_sample_attn_kernel, pages=pages, chains=chains, n_steps=n_steps, s_new=s_new),
        grid_spec=grid_spec,
        out_shape=jax.ShapeDtypeStruct((nb, s_new, ATTN_WIDTH), _BF16),
        compiler_params=pltpu.CompilerParams(
            dimension_semantics=("arbitrary", "arbitrary"), vmem_limit_bytes=VMEM_LIMIT),
        name="sample_attn",
    )(page_table, ql, qr, c_new, k_new, w_v, grp_g, cache_ckv, cache_krt)


def _mix_tail(y, h, a_n, cb_ref, lng_ref, lnb_ref, cg_ref, woa_ref, woc_ref):
    y = y + cb_ref[...]
    mu = jnp.mean(y, axis=-1, keepdims=True)
    yc = y - mu
    var = jnp.mean(yc * yc, axis=-1, keepdims=True)
    y = yc * lax.rsqrt(var + EPS) * lng_ref[...] + lnb_ref[...]
    y = y * _sigmoid(y)
    cv = _rms(y, cg_ref[...]).astype(_BF16)
    return h + _dot(a_n, woa_ref[...]) + _dot(cv, woc_ref[...])


def _conv_rows(ext_sc, cw_ref, r0):
    off = HALO - (CONV_WIDTH - 1)
    tiles = []
    for c0 in range(0, CONV_CH, LANES):
        cs = slice(c0, c0 + LANES)
        y = jnp.zeros((CONV_ROWS, LANES), _F32)
        for res in range(SUBLANES):
            taps = [k for k in range(CONV_WIDTH) if (off + k) % SUBLANES == res]
            n_rows = CONV_ROWS + (SUBLANES if res else 0)
            z = jnp.zeros((n_rows, LANES), _F32)
            for k in taps:
                base = r0 + off + k - res
                z = z + ext_sc[base:base + n_rows, cs] * cw_ref[k:k + 1, cs]
            y = y + z[res:res + CONV_ROWS]
        tiles.append(y)
    return jnp.concatenate(tiles, axis=-1)


def _mix_prompt_kernel(h_ref, a_ref, glu_ref, prev_ref, first_ref, cw_ref, cb_ref, lng_ref, lnb_ref,
                       cg_ref, woa_ref, woc_ref, o_ref, ext_sc, y_sc, *, tm):
    i = pl.program_id(1)
    ext_sc[:HALO, :] = jnp.where(i == 0, first_ref[...], prev_ref[...])
    ext_sc[HALO:, :] = glu_ref[...]
    for r0 in range(0, tm, CONV_ROWS):
        y_sc[r0:r0 + CONV_ROWS, :] = _conv_rows(ext_sc, cw_ref, r0)
    o_ref[...] = _mix_tail(y_sc[...], h_ref[...], a_ref[...], cb_ref, lng_ref, lnb_ref, cg_ref,
                           woa_ref, woc_ref)


def _mix_prompt(h, a_n, glu, first_halo, mw, *, batch, seq, tm):
    nt = seq // tm
    d = h.shape[1]
    hb = tm // HALO
    row = lambda b, i: (b * nt + i, 0)
    return pl.pallas_call(
        functools.partial(_mix_prompt_kernel, tm=tm),
        grid=(batch, nt),
        in_specs=[
            pl.BlockSpec((tm, d), row),
            pl.BlockSpec((tm, ATTN_WIDTH), row),
            pl.BlockSpec((tm, CONV_CH), row),
            pl.BlockSpec((HALO, CONV_CH), lambda b, i: (jnp.maximum((b * nt + i) * hb - 1, 0), 0)),
            _const_spec((HALO, CONV_CH)),
            _const_spec(mw["conv_w"].shape), _const_spec(mw["conv_b"].shape),
            _const_spec(mw["ln_g"].shape), _const_spec(mw["ln_b"].shape),
            _const_spec(mw["conv_grp"].shape),
            _const_spec(mw["w_out_a"].shape), _const_spec(mw["w_out_c"].shape),
        ],
        out_specs=pl.BlockSpec((tm, d), row),
        out_shape=jax.ShapeDtypeStruct(h.shape, _F32),
        scratch_shapes=[pltpu.VMEM((HALO + tm, CONV_CH), _F32),
                        pltpu.VMEM((tm, CONV_CH), _F32)],
        compiler_params=pltpu.CompilerParams(
            dimension_semantics=("arbitrary", "arbitrary"), vmem_limit_bytes=VMEM_LIMIT),
        name="mix_prompt",
    )(h, a_n, glu, glu, first_halo, mw["conv_w"], mw["conv_b"], mw["ln_g"], mw["ln_b"],
      mw["conv_grp"], mw["w_out_a"], mw["w_out_c"])


def _mix_sample_kernel(h_ref, a_ref, full_ref, cw_ref, cb_ref, lng_ref, lnb_ref,
                       cg_ref, woa_ref, woc_ref, o_ref, *, s_new):
    ys = []
    for t in range(s_new):
        y = jnp.zeros(full_ref.shape[1:], _F32)
        for k in range(CONV_WIDTH):
            y = y + full_ref[t + k] * cw_ref[k:k + 1, :]
        ys.append(y)
    y = jnp.concatenate(ys, axis=0)
    o_ref[...] = _mix_tail(y, h_ref[...], a_ref[...], cb_ref, lng_ref, lnb_ref, cg_ref, woa_ref, woc_ref)


def _mix_sample(h, a_n, full_t, mw, *, s_new):
    t, d = h.shape
    return pl.pallas_call(
        functools.partial(_mix_sample_kernel, s_new=s_new),
        grid=(1,),
        in_specs=[
            _const_spec((t, d)), _const_spec((t, ATTN_WIDTH)), _const_spec(full_t.shape),
            _const_spec(mw["conv_w"].shape), _const_spec(mw["conv_b"].shape),
            _const_spec(mw["ln_g"].shape), _const_spec(mw["ln_b"].shape),
            _const_spec(mw["conv_grp"].shape),
            _const_spec(mw["w_out_a"].shape), _const_spec(mw["w_out_c"].shape),
        ],
        out_specs=pl.BlockSpec((t, d), lambda i: (0, 0)),
        out_shape=jax.ShapeDtypeStruct((t, d), _F32),
        compiler_params=pltpu.CompilerParams(
            dimension_semantics=("arbitrary",), vmem_limit_bytes=VMEM_LIMIT),
        name="mix_sample",
    )(h, a_n, full_t, mw["conv_w"], mw["conv_b"], mw["ln_g"], mw["ln_b"],
      mw["conv_grp"], mw["w_out_a"], mw["w_out_c"])


def _rope_tables(pos):
    half = QK_ROPE_DIM // 2
    inv = ROPE_THETA ** (-jnp.arange(half, dtype=_F32) / half)
    ang = pos.astype(_F32)[:, None] * inv[None, :]
    cos = jnp.cos(ang)
    sin = jnp.sin(ang)
    n = pos.shape[0]
    cos2 = jnp.concatenate([cos, cos], axis=-1)
    sin2 = jnp.concatenate([sin, sin], axis=-1)
    pad_q = jnp.zeros((n, HEAD_SLOT - Q_HEAD_DIM), _F32)
    cosq = SCORE_SCALE * jnp.concatenate([jnp.ones((n, QK_NOPE_DIM), _F32), cos2, pad_q], axis=-1)
    sinq = SCORE_SCALE * jnp.concatenate([jnp.zeros((n, QK_NOPE_DIM), _F32), sin2, pad_q], axis=-1)
    pad_k = jnp.zeros((n, LANES - QK_ROPE_DIM), _F32)
    cosk = jnp.concatenate([cos2, pad_k], axis=-1)
    sink = jnp.concatenate([sin2, pad_k], axis=-1)
    return cosq, sinq, cosk, sink


def _rot_cols(w):
    half = QK_ROPE_DIM // 2
    return jnp.concatenate([-w[..., half:], w[..., :half]], axis=-1)


def _layer_weights(l, mix_norm, w_in, q_norm, w_uq, kv_norm, w_uk, w_uv):
    d = w_in.shape[1]
    win = w_in[l]
    o_kv = Q_RANK
    o_kr = o_kv + KV_RANK
    o_cv = o_kr + QK_ROPE_DIM
    w_kr = win[:, o_kr:o_cv]
    zpad = jnp.zeros((d, LANES - QK_ROPE_DIM), _F32)
    win_p = jnp.concatenate([win[:, :o_kr], win[:, o_cv:], w_kr, zpad, _rot_cols(w_kr), zpad], axis=-1)

    wq = w_uq[l].reshape(Q_RANK, N_HEADS, Q_HEAD_DIM)
    zq = jnp.zeros((Q_RANK, N_HEADS, HEAD_SLOT - Q_HEAD_DIM), _F32)
    w_qa = jnp.concatenate([wq, zq], axis=-1).reshape(Q_RANK, N_HEADS * HEAD_SLOT)
    w_qb = jnp.concatenate([jnp.zeros((Q_RANK, N_HEADS, QK_NOPE_DIM), _F32),
                            _rot_cols(wq[..., QK_NOPE_DIM:]), zq], axis=-1).reshape(Q_RANK, N_HEADS * HEAD_SLOT)

    wuk = w_uk[l]
    zk = jnp.zeros((KV_RANK, N_HEADS, HEAD_SLOT - QK_NOPE_DIM), _F32)
    w_k1 = jnp.concatenate([wuk, zk], axis=-1).reshape(KV_RANK, N_HEADS * HEAD_SLOT)
    eye = jnp.eye(QK_ROPE_DIM, dtype=_F32)
    place = jnp.concatenate([jnp.zeros((QK_ROPE_DIM, QK_NOPE_DIM), _F32), eye,
                             jnp.zeros((QK_ROPE_DIM, HEAD_SLOT - Q_HEAD_DIM), _F32)], axis=-1)
    w_k2 = jnp.concatenate([jnp.tile(place, (1, N_HEADS)),
                            jnp.zeros((LANES - QK_ROPE_DIM, N_HEADS * HEAD_SLOT), _F32)], axis=0)
    w_v = w_uv[l].reshape(KV_RANK, ATTN_WIDTH)

    w_vt = jnp.concatenate([w_uv[l].transpose(1, 2, 0), jnp.zeros((N_HEADS, VT_ONES, KV_RANK), _F32)],
                           axis=1).reshape(N_HEADS * VT_ROWS, KV_RANK)
    vt_bias = jnp.tile(jnp.concatenate([jnp.zeros((V_HEAD_DIM, 1), _F32), jnp.ones((VT_ONES, 1), _F32)],
                                       axis=0), (N_HEADS, 1))

    head_eye = jnp.eye(N_HEADS, dtype=_F32)[:, None, :, None]
    slot_abs = jnp.concatenate([wuk.transpose(1, 2, 0),
                                jnp.zeros((N_HEADS, HEAD_SLOT - QK_NOPE_DIM, KV_RANK), _F32)], axis=1)
    w_abs = (head_eye * slot_abs[:, :, None, :]).reshape(N_HEADS * HEAD_SLOT, N_HEADS * KV_RANK)
    w_qr = (head_eye * place.T[None, :, None, :]).reshape(N_HEADS * HEAD_SLOT, N_HEADS * QK_ROPE_DIM)

    bf = lambda x: x.astype(_BF16)
    return {
        "mix_norm": mix_norm[l][None], "w_in": bf(win_p), "q_norm": q_norm[l][None],
        "w_qa": bf(w_qa), "w_qb": bf(w_qb), "kv_norm": kv_norm[l][None],
        "w_k1": bf(w_k1), "w_k2": bf(w_k2), "w_v": bf(w_v), "w_vt": bf(w_vt), "vt_bias": vt_bias,
        "w_abs": bf(w_abs), "w_qr": bf(w_qr),
    }


def _pick_tile(n, prefer):
    for t in prefer:
        if n % t == 0:
            return t
    return n


def kernel(x_prompt, x_sample, cache_kv_latent, cache_k_rope, state_conv, page_table, meta_tokens,
           ffn1_norm, ffn1_w_gate, ffn1_w_up, ffn1_w_down, mix_norm, w_in, q_norm, w_uq, kv_norm,
           w_uk, w_uv, conv_w, conv_b, conv_ln_g, conv_ln_b, attn_grp_norm, conv_grp_norm, w_out,
           ffn2_norm, ffn2_w_gate, ffn2_w_up, ffn2_w_down, final_norm):
    b_p, s_p, d = x_prompt.shape
    b_s, s_s, _ = x_sample.shape
    depth = ffn1_norm.shape[0]
    assert depth == 1, "the meta rows' later-layer inputs are not carried; single layer only"
    n_pages = page_table.shape[1]
    past_len = n_pages * PAGE_SIZE
    assert CONV_WIDTH - 1 <= HALO and N_META <= HALO
    l = 0
    bf = lambda x: x.astype(_BF16)

    pw = _layer_weights(l, mix_norm, w_in, q_norm, w_uq, kv_norm, w_uk, w_uv)
    f1 = (ffn1_norm[l][None], bf(ffn1_w_gate[l]), bf(ffn1_w_up[l]), bf(ffn1_w_down[l]))
    f2 = (ffn2_norm[l][None], bf(ffn2_w_gate[l]), bf(ffn2_w_up[l]), bf(ffn2_w_down[l]))
    fin = final_norm[None]
    mw = {
        "conv_w": jnp.concatenate([conv_w[l], jnp.zeros((1, CONV_CH), _F32)], axis=0),
        "conv_b": conv_b[l][None], "ln_g": conv_ln_g[l][None], "ln_b": conv_ln_b[l][None],
        "conv_grp": conv_grp_norm[l][None],
        "w_out_a": bf(w_out[l][:ATTN_WIDTH]), "w_out_c": bf(w_out[l][ATTN_WIDTH:]),
    }

    tm_p = _pick_tile(s_p, (512, 256, 128))
    tq = _pick_tile(s_p, (512, 256, 128))
    tk = _pick_tile(tq, (256, 128))
    assert tm_p % CONV_ROWS == 0 and tm_p % HALO == 0

    h_m = _ffn(meta_tokens, *f1, fin, tm=N_META, final=False)
    tab_m = _rope_tables(jnp.arange(N_META, dtype=jnp.int32))
    ckv_m, kr_m, glu_m, _, k_m, vt_m = _proj(h_m, tab_m, 1, pw, tm=N_META, absorbed=False, tkv=N_META)
    first_halo = jnp.concatenate([jnp.zeros((HALO - N_META, CONV_CH), _F32), glu_m], axis=0)

    xp = x_prompt.reshape(b_p * s_p, d)
    h_p = _ffn(xp, *f1, fin, tm=tm_p, final=False)
    tab_p = _rope_tables(N_META + jnp.arange(s_p, dtype=jnp.int32))
    ckv_p, kr_p, glu_p, q_p, k_p, vt_p = _proj(h_p, tab_p, s_p // tm_p, pw, tm=tm_p, absorbed=False, tkv=tk)
    a_p = _prompt_attention(q_p, k_p, vt_p, k_m, vt_m[0], attn_grp_norm[l][:, None],
                            batch=b_p, seq=s_p, tq=tq, tk=tk)
    h2_p = _mix_prompt(h_p, a_p, glu_p, first_halo, mw, batch=b_p, seq=s_p, tm=tm_p)
    y_p = _ffn(h2_p, *f2, fin, tm=tm_p, final=True).reshape(b_p, s_p, d)

    n_s = b_s * s_s
    xs = x_sample.transpose(1, 0, 2).reshape(n_s, d)
    h_s = _ffn(xs, *f1, fin, tm=n_s, final=False)
    pos_s = past_len + jnp.repeat(jnp.arange(s_s, dtype=jnp.int32), b_s)
    ckv_s, kr_s, glu_s, ql_s, qr_s = _proj(h_s, _rope_tables(pos_s), 1, pw, tm=n_s, absorbed=True)
    seq_major = lambda x: x.reshape(s_s, b_s, -1).transpose(1, 0, 2)
    ckv_sb, kr_sb, glu_sb = seq_major(ckv_s), seq_major(kr_s), seq_major(glu_s)
    ql_b = seq_major(ql_s).reshape(b_s, s_s * N_HEADS, KV_RANK)
    qr_b = seq_major(qr_s).reshape(b_s, s_s * N_HEADS, QK_ROPE_DIM)
    per_step = _pick_tile(n_pages, (32, 16, 8, 4, 2))
    chains = _pick_tile(per_step, (4, 2))
    a_s = _sample_attention(page_table, ql_b, qr_b, ckv_sb, kr_sb, pw["w_v"], attn_grp_norm[l][None],
                            cache_kv_latent[l], jnp.swapaxes(cache_k_rope[l], 1, 2),
                            pages=per_step // chains, chains=chains)
    a_st = a_s.transpose(1, 0, 2).reshape(n_s, ATTN_WIDTH)
    full_t = jnp.concatenate([state_conv[l].transpose(1, 0, 2), glu_s.reshape(s_s, b_s, CONV_CH)], axis=0)
    h2_s = _mix_sample(h_s, a_st, full_t, mw, s_new=s_s)
    y_s = _ffn(h2_s, *f2, fin, tm=n_s, final=True).reshape(s_s, b_s, d).transpose(1, 0, 2)

    bcast = lambda x: jnp.broadcast_to(x[None], (b_p,) + x.shape)
    new_ckv_p = jnp.concatenate([bcast(ckv_m), ckv_p.reshape(b_p, s_p, KV_RANK)], axis=1)[None]
    new_kr_p = jnp.concatenate([bcast(kr_m), kr_p.reshape(b_p, s_p, QK_ROPE_DIM)], axis=1)[None]
    n_keep = CONV_WIDTH - 1
    new_conv_p = glu_p.reshape(b_p, s_p, CONV_CH)[:, s_p - n_keep:][None]
    new_conv_s = jnp.concatenate([state_conv[l], glu_sb], axis=1)[:, -n_keep:][None]
    return (y_p, y_s, new_ckv_p, new_kr_p, new_conv_p, ckv_sb[None], kr_sb[None], new_conv_s)
```

```python
import functools
import math

import jax
import jax.numpy as jnp
from jax import lax
from jax.experimental import pallas as pl
from jax.experimental.pallas import tpu as pltpu

N_META = 16
N_HEADS = 8
QK_NOPE_DIM = 64
QK_ROPE_DIM = 32
V_HEAD_DIM = 64
Q_RANK = 256
KV_RANK = 256
CONV_CH = 512
CONV_WIDTH = 31
ROPE_THETA = 10000.0
EPS = 1e-6
PAGE_SIZE = 128
Q_HEAD_DIM = QK_NOPE_DIM + QK_ROPE_DIM
ATTN_WIDTH = N_HEADS * V_HEAD_DIM
SOFTMAX_SCALE = Q_HEAD_DIM ** -0.5
SCORE_SCALE = SOFTMAX_SCALE * math.log2(math.e)

LANES = 128
SUBLANES = 8
HEAD_SLOT = LANES
HALO = 32
CONV_ROWS = 64
HEAD_LOOKAHEAD = 3
FFN_CHUNK_ELEMS = 512 * 1408
PAGE_RING = 3
VT_ONES = 16
VT_ROWS = V_HEAD_DIM + VT_ONES
VMEM_LIMIT = 56 * 1024 * 1024

_F32 = jnp.float32
_BF16 = jnp.bfloat16


def _const_spec(shape):
    nd = len(shape)
    return pl.BlockSpec(shape, lambda *_: (0,) * nd, pipeline_mode=pl.Buffered(1))


def _rms(x, g):
    return x * lax.rsqrt(jnp.mean(x * x, axis=-1, keepdims=True) + EPS) * g


def _sigmoid(x):
    return 1.0 / (1.0 + jnp.exp(-x))


def _dot(a, b):
    return jnp.dot(a, b, preferred_element_type=_F32)


def _dot_nt(a, b):
    return lax.dot_general(a, b, (((1,), (1,)), ((), ())), preferred_element_type=_F32)


def _ffn_kernel(x_ref, g_ref, wg_ref, wu_ref, wd_ref, fg_ref, o_ref, *, ff_chunks, final):
    x = x_ref[...]
    xn = _rms(x, g_ref[...]).astype(_BF16)
    acc = jnp.zeros_like(x)
    for c0, cw in ff_chunks:
        gate = _dot(xn, wg_ref[:, c0:c0 + cw])
        up = _dot(xn, wu_ref[:, c0:c0 + cw])
        act = (gate * _sigmoid(gate) * up).astype(_BF16)
        acc = acc + _dot(act, wd_ref[c0:c0 + cw, :])
    y = x + 0.5 * acc
    if final:
        y = _rms(y, fg_ref[...])
    o_ref[...] = y


def _ffn(x, norm_g, wg, wu, wd, final_g, *, tm, final):
    t, d = x.shape
    d_ff = wg.shape[1]
    n_lane_tiles = d_ff // LANES
    n_chunks = max(1, min(n_lane_tiles, -(-tm * d_ff // FFN_CHUNK_ELEMS)))
    bounds = [(n_lane_tiles * c // n_chunks) * LANES for c in range(n_chunks + 1)]
    ff_chunks = tuple((bounds[c], bounds[c + 1] - bounds[c]) for c in range(n_chunks))
    return pl.pallas_call(
        functools.partial(_ffn_kernel, ff_chunks=ff_chunks, final=final),
        grid=(t // tm,),
        in_specs=[
            pl.BlockSpec((tm, d), lambda i: (i, 0)),
            _const_spec((1, d)),
            _const_spec((d, d_ff)),
            _const_spec((d, d_ff)),
            _const_spec((d_ff, d)),
            _const_spec((1, d)),
        ],
        out_specs=pl.BlockSpec((tm, d), lambda i: (i, 0)),
        out_shape=jax.ShapeDtypeStruct((t, d), _F32),
        compiler_params=pltpu.CompilerParams(
            dimension_semantics=("arbitrary",), vmem_limit_bytes=VMEM_LIMIT),
        name="ffn_final" if final else "ffn",
    )(x, norm_g, wg, wu, wd, final_g)


def _proj_kernel(h_ref, cosq_ref, sinq_ref, cosk_ref, sink_ref,
                 mixg_ref, win_ref, qg_ref, wqa_ref, wqb_ref, kvg_ref, *rest, absorbed):
    if absorbed:
        wabs_ref, wqr_ref, ckv_ref, kr_ref, glu_ref, ql_ref, qr_ref = rest
    else:
        wk1_ref, wvt_ref, vtb_ref, ckv_ref, kr_ref, glu_ref, q_ref, k_ref, vt_ref = rest
    n = _rms(h_ref[...], mixg_ref[...]).astype(_BF16)
    z = _dot(n, win_ref[...])
    o_kv = Q_RANK
    o_ca = o_kv + KV_RANK
    o_cb = o_ca + CONV_CH
    o_ka = o_cb + CONV_CH
    o_kb = o_ka + LANES

    glu_ref[...] = z[:, o_ca:o_cb] * _sigmoid(z[:, o_cb:o_ka])

    c = _rms(z[:, o_kv:o_ca], kvg_ref[...])
    ckv_ref[...] = c
    c_bf = c.astype(_BF16)
    kr = z[:, o_ka:o_kb] * cosk_ref[...] + z[:, o_kb:o_kb + LANES] * sink_ref[...]
    kr_ref[...] = kr[:, :QK_ROPE_DIM]

    qn = _rms(z[:, :Q_RANK], qg_ref[...]).astype(_BF16)
    qa = _dot(qn, wqa_ref[...])
    qb = _dot(qn, wqb_ref[...])
    cosq = cosq_ref[...]
    sinq = sinq_ref[...]
    q_heads = []
    for h in range(N_HEADS):
        sl = slice(h * HEAD_SLOT, (h + 1) * HEAD_SLOT)
        q_heads.append((qa[:, sl] * cosq + qb[:, sl] * sinq).astype(_BF16))
    q = jnp.concatenate(q_heads, axis=-1)
    if absorbed:
        ql_ref[...] = _dot(q, wabs_ref[...]).astype(_BF16)
        qr_ref[...] = _dot(q, wqr_ref[...]).astype(_BF16)
    else:
        q_ref[...] = q
        k_nope = _dot(c_bf, wk1_ref[...])
        kr_slot = pltpu.roll(kr, QK_NOPE_DIM, 1)
        k_ref[...] = jnp.concatenate(
            [(k_nope[:, h * HEAD_SLOT:(h + 1) * HEAD_SLOT] + kr_slot).astype(_BF16) for h in range(N_HEADS)],
            axis=-1)
        vt = (_dot_nt(wvt_ref[...], c_bf) + vtb_ref[...]).astype(_BF16)
        tkv = vt_ref.shape[2]
        for blk in range(vt_ref.shape[0]):
            vt_ref[blk] = vt[:, blk * tkv:(blk + 1) * tkv]


def _proj(h, tables, n_table_blocks, pw, *, tm, absorbed, tkv=None):
    t, d = h.shape
    cosq, sinq, cosk, sink = tables
    row = lambda i: (i, 0)
    tab = lambda i: (i % n_table_blocks, 0)
    in_specs = [
        pl.BlockSpec((tm, d), row),
        pl.BlockSpec((tm, LANES), tab), pl.BlockSpec((tm, LANES), tab),
        pl.BlockSpec((tm, LANES), tab), pl.BlockSpec((tm, LANES), tab),
        _const_spec(pw["mix_norm"].shape), _const_spec(pw["w_in"].shape),
        _const_spec(pw["q_norm"].shape), _const_spec(pw["w_qa"].shape),
        _const_spec(pw["w_qb"].shape), _const_spec(pw["kv_norm"].shape),
    ]
    args = [h, cosq, sinq, cosk, sink, pw["mix_norm"], pw["w_in"], pw["q_norm"],
            pw["w_qa"], pw["w_qb"], pw["kv_norm"]]
    out_shape = [jax.ShapeDtypeStruct((t, KV_RANK), _F32),
                 jax.ShapeDtypeStruct((t, QK_ROPE_DIM), _F32),
                 jax.ShapeDtypeStruct((t, CONV_CH), _F32)]
    out_specs = [pl.BlockSpec((tm, KV_RANK), row), pl.BlockSpec((tm, QK_ROPE_DIM), row),
                 pl.BlockSpec((tm, CONV_CH), row)]
    if absorbed:
        extra = ["w_abs", "w_qr"]
        widths = [N_HEADS * KV_RANK, N_HEADS * QK_ROPE_DIM]
    else:
        extra = ["w_k1", "w_vt", "vt_bias"]
        widths = [N_HEADS * HEAD_SLOT, N_HEADS * HEAD_SLOT]
    for name in extra:
        in_specs.append(_const_spec(pw[name].shape))
        args.append(pw[name])
    for w in widths:
        out_shape.append(jax.ShapeDtypeStruct((t, w), _BF16))
        out_specs.append(pl.BlockSpec((tm, w), row))
    if not absorbed:
        vr = N_HEADS * VT_ROWS
        out_shape.append(jax.ShapeDtypeStruct((t // tkv, vr, tkv), _BF16))
        out_specs.append(pl.BlockSpec((tm // tkv, vr, tkv), lambda i: (i, 0, 0)))
    return pl.pallas_call(
        functools.partial(_proj_kernel, absorbed=absorbed),
        grid=(t // tm,),
        in_specs=in_specs,
        out_specs=out_specs,
        out_shape=out_shape,
        compiler_params=pltpu.CompilerParams(
            dimension_semantics=("arbitrary",), vmem_limit_bytes=VMEM_LIMIT),
        name="proj_absorbed" if absorbed else "proj",
    )(*args)


def _prompt_attn_kernel(q_ref, k_ref, vt_ref, km_ref, vtm_ref, g_ref, o_ref, m_sc, acc_sc, *, tq, tk):
    i = pl.program_id(1)
    n_full = i * (tq // tk)

    def q_slot(h):
        return slice(h * HEAD_SLOT, (h + 1) * HEAD_SLOT)

    def v_rows(h):
        return slice(h * VT_ROWS, (h + 1) * VT_ROWS)

    def update(h, s, vt_t, c0, mask=None, first=False):
        rs = v_rows(h)
        if mask is not None:
            s = jnp.where(mask, s, -jnp.inf)
        s_max = jnp.max(s, axis=0, keepdims=True)
        if first:
            m_new = s_max
            acc_sc[rs, c0:] = _dot(vt_t, jnp.exp2(s - m_new).astype(_BF16))
        else:
            m_old = m_sc[h, :, c0:]
            m_new = jnp.maximum(m_old, s_max)
            alpha = jnp.exp2(m_old - m_new)
            p = jnp.exp2(s - m_new)
            acc_sc[rs, c0:] = acc_sc[rs, c0:] * alpha + _dot(vt_t, p.astype(_BF16))
        m_sc[h, :, c0:] = m_new

    def sweep(scores, values, c0=0, mask=None, first=False):
        pending = [scores(h) for h in range(HEAD_LOOKAHEAD)]
        for h in range(N_HEADS):
            s = pending.pop(0)
            if h + HEAD_LOOKAHEAD < N_HEADS:
                pending.append(scores(h + HEAD_LOOKAHEAD))
            update(h, s, values(h), c0, mask, first)

    def tile(j, c0=0, mask=None):
        ks = pl.ds(pl.multiple_of(j * tk, tk), tk)
        sweep(lambda h: _dot_nt(k_ref[ks, q_slot(h)], q_ref[c0:, q_slot(h)]),
              lambda h: vt_ref[j, v_rows(h), :], c0, mask)

    sweep(lambda h: _dot_nt(km_ref[:, q_slot(h)], q_ref[:, q_slot(h)]),
          lambda h: vtm_ref[v_rows(h), :], first=True)

    def full_tile(j, carry):
        tile(j)
        return carry

    lax.fori_loop(0, n_full, full_tile, 0)
    for d in range(tq // tk):
        c0 = d * tk
        key_r = lax.broadcasted_iota(jnp.int32, (tk, tq - c0), 0)
        qry_c = lax.broadcasted_iota(jnp.int32, (tk, tq - c0), 1)
        tile(n_full + d, c0, mask=key_r <= qry_c)

    heads = []
    for h in range(N_HEADS):
        r0 = h * VT_ROWS
        inv_l = 1.0 / acc_sc[r0 + V_HEAD_DIM:r0 + V_HEAD_DIM + 1, :]
        heads.append(acc_sc[r0:r0 + V_HEAD_DIM, :] * inv_l)
    a_t = jnp.concatenate(heads, axis=0)
    ms = jnp.mean(a_t * a_t, axis=0, keepdims=True)
    a_n = a_t * lax.rsqrt(ms + EPS) * g_ref[...]
    o_ref[...] = a_n.T.astype(_BF16)


def _prompt_attention(q, k, vt, k_meta, vt_meta, grp_g_col, *, batch, seq, tq, tk):
    nq = seq // tq
    nk = seq // tk
    kw = N_HEADS * HEAD_SLOT
    vr = N_HEADS * VT_ROWS
    return pl.pallas_call(
        functools.partial(_prompt_attn_kernel, tq=tq, tk=tk),
        grid=(batch, nq),
        in_specs=[
            pl.BlockSpec((tq, kw), lambda b, i: (b * nq + i, 0)),
            pl.BlockSpec((seq, kw), lambda b, i: (b, 0)),
            pl.BlockSpec((nk, vr, tk), lambda b, i: (b, 0, 0)),
            _const_spec(k_meta.shape),
            _const_spec(vt_meta.shape),
            _const_spec((ATTN_WIDTH, 1)),
        ],
        out_specs=pl.BlockSpec((tq, ATTN_WIDTH), lambda b, i: (b * nq + i, 0)),
        out_shape=jax.ShapeDtypeStruct((batch * seq, ATTN_WIDTH), _BF16),
        scratch_shapes=[
            pltpu.VMEM((N_HEADS, 1, tq), _F32),
            pltpu.VMEM((vr, tq), _F32),
        ],
        compiler_params=pltpu.CompilerParams(
            dimension_semantics=("arbitrary", "arbitrary"), vmem_limit_bytes=VMEM_LIMIT),
        name="prompt_attn",
    )(q, k, vt, k_meta, vt_meta, grp_g_col)


def _sample_attn_kernel(pt_ref, ql_ref, qr_ref, cn_ref, kn_ref, wv_ref, g_ref, ckv_hbm, krt_hbm,
                        o_ref, cbuf, kbuf, sem, m_sc, l_sc, acc_sc, *, pages, chains, n_steps, s_new):
    b = pl.program_id(0)
    j = pl.program_id(1)
    nb = pl.num_programs(0)
    step = b * n_steps + j
    n_slots = cbuf.shape[0]
    slot = step % n_slots
    rows = s_new * N_HEADS
    per_step = pages * chains

    def page_copy(page, p, sl):
        return (pltpu.make_async_copy(ckv_hbm.at[page], cbuf.at[sl, p], sem.at[0, sl]),
                pltpu.make_async_copy(krt_hbm.at[page], kbuf.at[sl, p], sem.at[1, sl]))

    def start_pages(st):
        bb, jj, sl = st // n_steps, st % n_steps, st % n_slots
        for p in range(per_step):
            for cp in page_copy(pt_ref[bb, jj * per_step + p], p, sl):
                cp.start()

    @pl.when(step == 0)
    def _():
        for st in range(n_slots - 1):
            start_pages(st)

    @pl.when(step + n_slots - 1 < nb * n_steps)
    def _():
        start_pages(step + n_slots - 1)

    @pl.when(j == 0)
    def _():
        m_sc[...] = jnp.full(m_sc.shape, -jnp.inf, _F32)
        l_sc[...] = jnp.zeros(l_sc.shape, _F32)
        acc_sc[...] = jnp.zeros(acc_sc.shape, _F32)

    for p in range(per_step):
        for cp in page_copy(0, p, slot):
            cp.wait()

    ql = ql_ref[0]
    qr = qr_ref[0]
    n_keys = pages * PAGE_SIZE
    latents, scores = [], []
    for c in range(chains):
        c_bf = cbuf[slot, c * pages:(c + 1) * pages].reshape(n_keys, KV_RANK).astype(_BF16)
        krt = jnp.concatenate([kbuf[slot, c * pages + p] for p in range(pages)], axis=1).astype(_BF16)
        latents.append(c_bf)
        scores.append(_dot_nt(ql, c_bf) + _dot(qr, krt))
    for c in range(chains):
        c_bf, s = latents[c], scores[c]
        m = m_sc[c]
        m_new = jnp.maximum(m, jnp.max(s, axis=-1, keepdims=True))
        alpha = jnp.exp2(m - m_new)
        p = jnp.exp2(s - m_new)
        l_sc[c] = l_sc[c] * alpha + jnp.sum(p, axis=-1, keepdims=True)
        acc_sc[c] = acc_sc[c] * alpha + _dot(p.astype(_BF16), c_bf)
        m_sc[c] = m_new

    @pl.when(j == n_steps - 1)
    def _():
        qlf = ql.astype(_F32)
        qrf = qr.astype(_F32)
        tok = lax.broadcasted_iota(jnp.int32, (rows, 1), 0) // N_HEADS
        cn = cn_ref[0].astype(_BF16).astype(_F32)
        kn = kn_ref[0].astype(_BF16).astype(_F32)
        s_cols = []
        for t in range(s_new):
            st = (jnp.sum(qlf * cn[t:t + 1, :], axis=-1, keepdims=True)
                  + jnp.sum(qrf * kn[t:t + 1, :], axis=-1, keepdims=True))
            s_cols.append(jnp.where(tok >= t, st, -jnp.inf))
        m_new = m_sc[0]
        for c in range(1, chains):
            m_new = jnp.maximum(m_new, m_sc[c])
        for st in s_cols:
            m_new = jnp.maximum(m_new, st)
        l = jnp.zeros((rows, 1), _F32)
        acc = jnp.zeros((rows, KV_RANK), _F32)
        for c in range(chains):
            alpha = jnp.exp2(m_sc[c] - m_new)
            l = l + l_sc[c] * alpha
            acc = acc + acc_sc[c] * alpha
        for t in range(s_new):
            pt = jnp.exp2(s_cols[t] - m_new)
            l = l + pt
            acc = acc + pt.astype(_BF16).astype(_F32) * cn[t:t + 1, :]
        o = (acc * (1.0 / l)).astype(_BF16)
        full = _dot(o, wv_ref[...])
        head = lax.broadcasted_iota(jnp.int32, (rows, ATTN_WIDTH), 0) % N_HEADS
        lane_head = lax.broadcasted_iota(jnp.int32, (rows, ATTN_WIDTH), 1) // V_HEAD_DIM
        full = jnp.where(head == lane_head, full, 0.0)
        a = jnp.concatenate(
            [jnp.sum(full[t * N_HEADS:(t + 1) * N_HEADS], axis=0, keepdims=True) for t in range(s_new)],
            axis=0)
        o_ref[0] = _rms(a, g_ref[...]).astype(_BF16)


def _sample_attention(page_table, ql, qr, c_new, k_new, w_v, grp_g, cache_ckv, cache_krt, *, pages, chains):
    nb, n_pages = page_table.shape
    per_step = pages * chains
    n_steps = n_pages // per_step
    s_new = c_new.shape[1]
    rows = s_new * N_HEADS
    grid_spec = pltpu.PrefetchScalarGridSpec(
        num_scalar_prefetch=1,
        grid=(nb, n_steps),
        in_specs=[
            pl.BlockSpec((1, rows, KV_RANK), lambda b, j, pt: (b, 0, 0)),
            pl.BlockSpec((1, rows, QK_ROPE_DIM), lambda b, j, pt: (b, 0, 0)),
            pl.BlockSpec((1, s_new, KV_RANK), lambda b, j, pt: (b, 0, 0)),
            pl.BlockSpec((1, s_new, QK_ROPE_DIM), lambda b, j, pt: (b, 0, 0)),
            _const_spec((KV_RANK, ATTN_WIDTH)),
            _const_spec((1, ATTN_WIDTH)),
            pl.BlockSpec(memory_space=pl.ANY),
            pl.BlockSpec(memory_space=pl.ANY),
        ],
        out_specs=pl.BlockSpec((1, s_new, ATTN_WIDTH), lambda b, j, pt: (b, 0, 0)),
        scratch_shapes=[
            pltpu.VMEM((PAGE_RING, per_step, PAGE_SIZE, KV_RANK), _F32),
            pltpu.VMEM((PAGE_RING, per_step, QK_ROPE_DIM, PAGE_SIZE), _F32),
            pltpu.SemaphoreType.DMA((2, PAGE_RING)),
            pltpu.VMEM((chains, rows, 1), _F32),
            pltpu.VMEM((chains, rows, 1), _F32),
            pltpu.VMEM((chains, rows, KV_RANK), _F32),
        ],
    )
    return pl.pallas_call(
        functools.partial(_sample_attn_kernel, pages=pages, chains=chains, n_steps=n_steps, s_new=s_new),
        grid_spec=grid_spec,
        out_shape=jax.ShapeDtypeStruct((nb, s_new, ATTN_WIDTH), _BF16),
        compiler_params=pltpu.CompilerParams(
            dimension_semantics=("arbitrary", "arbitrary"), vmem_limit_bytes=VMEM_LIMIT),
        name="sample_attn",
    )(page_table, ql, qr, c_new, k_new, w_v, grp_g, cache_ckv, cache_krt)


def _mix_tail(y, h, a_n, cb_ref, lng_ref, lnb_ref, cg_ref, woa_ref, woc_ref):
    y = y + cb_ref[...]
    mu = jnp.mean(y, axis=-1, keepdims=True)
    yc = y - mu
    var = jnp.mean(yc * yc, axis=-1, keepdims=True)
    y = yc * lax.rsqrt(var + EPS) * lng_ref[...] + lnb_ref[...]
    y = y * _sigmoid(y)
    cv = _rms(y, cg_ref[...]).astype(_BF16)
    return h + _dot(a_n, woa_ref[...]) + _dot(cv, woc_ref[...])


def _shift_copies(ext_sc, sh_sc):
    n = sh_sc.shape[1]
    for r in range(1, SUBLANES):
        sh_sc[r - 1] = ext_sc[r:r + n, :]


def _conv_rows(ext_sc, sh_sc, cw_ref, r0):
    off = HALO - (CONV_WIDTH - 1)
    tiles = []
    for c0 in range(0, CONV_CH, LANES):
        cs = slice(c0, c0 + LANES)
        y = jnp.zeros((CONV_ROWS, LANES), _F32)
        for k in range(CONV_WIDTH):
            res = (off + k) % SUBLANES
            rows = slice(r0 + off + k - res, r0 + off + k - res + CONV_ROWS)
            src = ext_sc[rows, cs] if res == 0 else sh_sc[res - 1, rows, cs]
            y = y + src * cw_ref[k:k + 1, cs]
        tiles.append(y)
    return jnp.concatenate(tiles, axis=-1)


def _mix_prompt_kernel(h_ref, a_ref, glu_ref, prev_ref, first_ref, cw_ref, cb_ref, lng_ref, lnb_ref,
                       cg_ref, woa_ref, woc_ref, o_ref, ext_sc, sh_sc, y_sc, *, tm):
    i = pl.program_id(1)
    ext_sc[:HALO, :] = jnp.where(i == 0, first_ref[...], prev_ref[...])
    ext_sc[HALO:, :] = glu_ref[...]
    _shift_copies(ext_sc, sh_sc)
    for r0 in range(0, tm, CONV_ROWS):
        y_sc[r0:r0 + CONV_ROWS, :] = _conv_rows(ext_sc, sh_sc, cw_ref, r0)
    o_ref[...] = _mix_tail(y_sc[...], h_ref[...], a_ref[...], cb_ref, lng_ref, lnb_ref, cg_ref,
                           woa_ref, woc_ref)


def _mix_prompt(h, a_n, glu, first_halo, mw, *, batch, seq, tm):
    nt = seq // tm
    d = h.shape[1]
    hb = tm // HALO
    row = lambda b, i: (b * nt + i, 0)
    return pl.pallas_call(
        functools.partial(_mix_prompt_kernel, tm=tm),
        grid=(batch, nt),
        in_specs=[
            pl.BlockSpec((tm, d), row),
            pl.BlockSpec((tm, ATTN_WIDTH), row),
            pl.BlockSpec((tm, CONV_CH), row),
            pl.BlockSpec((HALO, CONV_CH), lambda b, i: (jnp.maximum((b * nt + i) * hb - 1, 0), 0)),
            _const_spec((HALO, CONV_CH)),
            _const_spec(mw["conv_w"].shape), _const_spec(mw["conv_b"].shape),
            _const_spec(mw["ln_g"].shape), _const_spec(mw["ln_b"].shape),
            _const_spec(mw["conv_grp"].shape),
            _const_spec(mw["w_out_a"].shape), _const_spec(mw["w_out_c"].shape),
        ],
        out_specs=pl.BlockSpec((tm, d), row),
        out_shape=jax.ShapeDtypeStruct(h.shape, _F32),
        scratch_shapes=[pltpu.VMEM((HALO + tm, CONV_CH), _F32),
                        pltpu.VMEM((SUBLANES - 1, HALO + tm - SUBLANES, CONV_CH), _F32),
                        pltpu.VMEM((tm, CONV_CH), _F32)],
        compiler_params=pltpu.CompilerParams(
            dimension_semantics=("arbitrary", "arbitrary"), vmem_limit_bytes=VMEM_LIMIT),
        name="mix_prompt",
    )(h, a_n, glu, glu, first_halo, mw["conv_w"], mw["conv_b"], mw["ln_g"], mw["ln_b"],
      mw["conv_grp"], mw["w_out_a"], mw["w_out_c"])


def _mix_sample_kernel(h_ref, a_ref, full_ref, cw_ref, cb_ref, lng_ref, lnb_ref,
                       cg_ref, woa_ref, woc_ref, o_ref, *, s_new):
    ys = []
    for t in range(s_new):
        y = jnp.zeros(full_ref.shape[1:], _F32)
        for k in range(CONV_WIDTH):
            y = y + full_ref[t + k] * cw_ref[k:k + 1, :]
        ys.append(y)
    y = jnp.concatenate(ys, axis=0)
    o_ref[...] = _mix_tail(y, h_ref[...], a_ref[...], cb_ref, lng_ref, lnb_ref, cg_ref, woa_ref, woc_ref)


def _mix_sample(h, a_n, full_t, mw, *, s_new):
    t, d = h.shape
    return pl.pallas_call(
        functools.partial(_mix_sample_kernel, s_new=s_new),
        grid=(1,),
        in_specs=[
            _const_spec((t, d)), _const_spec((t, ATTN_WIDTH)), _const_spec(full_t.shape),
            _const_spec(mw["conv_w"].shape), _const_spec(mw["conv_b"].shape),
            _const_spec(mw["ln_g"].shape), _const_spec(mw["ln_b"].shape),
            _const_spec(mw["conv_grp"].shape),
            _const_spec(mw["w_out_a"].shape), _const_spec(mw["w_out_c"].shape),
        ],
        out_specs=pl.BlockSpec((t, d), lambda i: (0, 0)),
        out_shape=jax.ShapeDtypeStruct((t, d), _F32),
        compiler_params=pltpu.CompilerParams(
            dimension_semantics=("arbitrary",), vmem_limit_bytes=VMEM_LIMIT),
        name="mix_sample",
    )(h, a_n, full_t, mw["conv_w"], mw["conv_b"], mw["ln_g"], mw["ln_b"],
      mw["conv_grp"], mw["w_out_a"], mw["w_out_c"])


def _rope_tables(pos):
    half = QK_ROPE_DIM // 2
    inv = ROPE_THETA ** (-jnp.arange(half, dtype=_F32) / half)
    ang = pos.astype(_F32)[:, None] * inv[None, :]
    cos = jnp.cos(ang)
    sin = jnp.sin(ang)
    n = pos.shape[0]
    cos2 = jnp.concatenate([cos, cos], axis=-1)
    sin2 = jnp.concatenate([sin, sin], axis=-1)
    pad_q = jnp.zeros((n, HEAD_SLOT - Q_HEAD_DIM), _F32)
    cosq = SCORE_SCALE * jnp.concatenate([jnp.ones((n, QK_NOPE_DIM), _F32), cos2, pad_q], axis=-1)
    sinq = SCORE_SCALE * jnp.concatenate([jnp.zeros((n, QK_NOPE_DIM), _F32), sin2, pad_q], axis=-1)
    pad_k = jnp.zeros((n, LANES - QK_ROPE_DIM), _F32)
    cosk = jnp.concatenate([cos2, pad_k], axis=-1)
    sink = jnp.concatenate([sin2, pad_k], axis=-1)
    return cosq, sinq, cosk, sink


def _rot_cols(w):
    half = QK_ROPE_DIM // 2
    return jnp.concatenate([-w[..., half:], w[..., :half]], axis=-1)


def _layer_weights(l, mix_norm, w_in, q_norm, w_uq, kv_norm, w_uk, w_uv):
    d = w_in.shape[1]
    win = w_in[l]
    o_kv = Q_RANK
    o_kr = o_kv + KV_RANK
    o_cv = o_kr + QK_ROPE_DIM
    w_kr = win[:, o_kr:o_cv]
    zpad = jnp.zeros((d, LANES - QK_ROPE_DIM), _F32)
    win_p = jnp.concatenate([win[:, :o_kr], win[:, o_cv:], w_kr, zpad, _rot_cols(w_kr), zpad], axis=-1)

    wq = w_uq[l].reshape(Q_RANK, N_HEADS, Q_HEAD_DIM)
    zq = jnp.zeros((Q_RANK, N_HEADS, HEAD_SLOT - Q_HEAD_DIM), _F32)
    w_qa = jnp.concatenate([wq, zq], axis=-1).reshape(Q_RANK, N_HEADS * HEAD_SLOT)
    w_qb = jnp.concatenate([jnp.zeros((Q_RANK, N_HEADS, QK_NOPE_DIM), _F32),
                            _rot_cols(wq[..., QK_NOPE_DIM:]), zq], axis=-1).reshape(Q_RANK, N_HEADS * HEAD_SLOT)

    wuk = w_uk[l]
    zk = jnp.zeros((KV_RANK, N_HEADS, HEAD_SLOT - QK_NOPE_DIM), _F32)
    w_k1 = jnp.concatenate([wuk, zk], axis=-1).reshape(KV_RANK, N_HEADS * HEAD_SLOT)
    eye = jnp.eye(QK_ROPE_DIM, dtype=_F32)
    place = jnp.concatenate([jnp.zeros((QK_ROPE_DIM, QK_NOPE_DIM), _F32), eye,
                             jnp.zeros((QK_ROPE_DIM, HEAD_SLOT - Q_HEAD_DIM), _F32)], axis=-1)
    w_v =w_uv[l].reshape(KV_RANK, ATTN_WIDTH)

    w_vt = jnp.concatenate([w_uv[l].transpose(1, 2, 0), jnp.zeros((N_HEADS, VT_ONES, KV_RANK), _F32)],
                           axis=1).reshape(N_HEADS * VT_ROWS, KV_RANK)
    vt_bias = jnp.tile(jnp.concatenate([jnp.zeros((V_HEAD_DIM, 1), _F32), jnp.ones((VT_ONES, 1), _F32)],
                                       axis=0), (N_HEADS, 1))

    head_eye = jnp.eye(N_HEADS, dtype=_F32)[:, None, :, None]
    slot_abs = jnp.concatenate([wuk.transpose(1, 2, 0),
                                jnp.zeros((N_HEADS, HEAD_SLOT - QK_NOPE_DIM, KV_RANK), _F32)], axis=1)
    w_abs = (head_eye * slot_abs[:, :, None, :]).reshape(N_HEADS * HEAD_SLOT, N_HEADS * KV_RANK)
    w_qr = (head_eye * place.T[None, :, None, :]).reshape(N_HEADS * HEAD_SLOT, N_HEADS * QK_ROPE_DIM)

    bf = lambda x: x.astype(_BF16)
    return {
        "mix_norm": mix_norm[l][None], "w_in": bf(win_p), "q_norm": q_norm[l][None],
        "w_qa": bf(w_qa), "w_qb": bf(w_qb), "kv_norm": kv_norm[l][None],
        "w_k1": bf(w_k1), "w_v": bf(w_v), "w_vt": bf(w_vt), "vt_bias": vt_bias,
        "w_abs": bf(w_abs), "w_qr": bf(w_qr),
    }


def _pick_tile(n, prefer):
    for t in prefer:
        if n % t == 0:
            return t
    return n


def kernel(x_prompt, x_sample, cache_kv_latent, cache_k_rope, state_conv, page_table, meta_tokens,
           ffn1_norm, ffn1_w_gate, ffn1_w_up, ffn1_w_down, mix_norm, w_in, q_norm, w_uq, kv_norm,
           w_uk, w_uv, conv_w, conv_b, conv_ln_g, conv_ln_b, attn_grp_norm, conv_grp_norm, w_out,
           ffn2_norm, ffn2_w_gate, ffn2_w_up, ffn2_w_down, final_norm):
    b_p, s_p, d = x_prompt.shape
    b_s, s_s, _ = x_sample.shape
    depth = ffn1_norm.shape[0]
    assert depth == 1, "the meta rows' later-layer inputs are not carried; single layer only"
    n_pages = page_table.shape[1]
    past_len = n_pages * PAGE_SIZE
    assert CONV_WIDTH - 1 <= HALO and N_META <= HALO
    l = 0
    bf = lambda x: x.astype(_BF16)

    pw = _layer_weights(l, mix_norm, w_in, q_norm, w_uq, kv_norm, w_uk, w_uv)
    f1 = (ffn1_norm[l][None], bf(ffn1_w_gate[l]), bf(ffn1_w_up[l]), bf(ffn1_w_down[l]))
    f2 = (ffn2_norm[l][None], bf(ffn2_w_gate[l]), bf(ffn2_w_up[l]), bf(ffn2_w_down[l]))
    fin = final_norm[None]
    mw = {
        "conv_w": jnp.concatenate([conv_w[l], jnp.zeros((1, CONV_CH), _F32)], axis=0),
        "conv_b": conv_b[l][None], "ln_g": conv_ln_g[l][None], "ln_b": conv_ln_b[l][None],
        "conv_grp": conv_grp_norm[l][None],
        "w_out_a": bf(w_out[l][:ATTN_WIDTH]), "w_out_c": bf(w_out[l][ATTN_WIDTH:]),
    }

    tm_p = _pick_tile(s_p, (512, 256, 128))
    tq = _pick_tile(s_p, (512, 256, 128))
    tk = _pick_tile(tq, (256, 128))
    assert tm_p % CONV_ROWS == 0 and tm_p % HALO == 0

    h_m = _ffn(meta_tokens, *f1, fin, tm=N_META, final=False)
    tab_m = _rope_tables(jnp.arange(N_META, dtype=jnp.int32))
    ckv_m, kr_m, glu_m, _, k_m, vt_m = _proj(h_m, tab_m, 1, pw, tm=N_META, absorbed=False, tkv=N_META)
    first_halo = jnp.concatenate([jnp.zeros((HALO - N_META, CONV_CH), _F32), glu_m], axis=0)

    xp = x_prompt.reshape(b_p * s_p, d)
    h_p = _ffn(xp, *f1, fin, tm=tm_p, final=False)
    tab_p = _rope_tables(N_META + jnp.arange(s_p, dtype=jnp.int32))
    ckv_p, kr_p, glu_p, q_p, k_p, vt_p = _proj(h_p, tab_p, s_p // tm_p, pw, tm=tm_p, absorbed=False, tkv=tk)
    a_p = _prompt_attention(q_p, k_p, vt_p, k_m, vt_m[0], attn_grp_norm[l][:, None],
                            batch=b_p, seq=s_p, tq=tq, tk=tk)
    h2_p = _mix_prompt(h_p, a_p, glu_p, first_halo, mw, batch=b_p, seq=s_p, tm=tm_p)
    y_p = _ffn(h2_p, *f2, fin, tm=tm_p, final=True).reshape(b_p, s_p, d)

    n_s = b_s * s_s
    xs = x_sample.transpose(1, 0, 2).reshape(n_s, d)
    h_s = _ffn(xs, *f1, fin, tm=n_s, final=False)
    pos_s = past_len + jnp.repeat(jnp.arange(s_s, dtype=jnp.int32), b_s)
    ckv_s, kr_s, glu_s, ql_s, qr_s = _proj(h_s, _rope_tables(pos_s), 1, pw, tm=n_s, absorbed=True)
    seq_major = lambda x: x.reshape(s_s, b_s, -1).transpose(1, 0, 2)
    ckv_sb, kr_sb, glu_sb = seq_major(ckv_s), seq_major(kr_s), seq_major(glu_s)
    ql_b = seq_major(ql_s).reshape(b_s, s_s * N_HEADS, KV_RANK)
    qr_b = seq_major(qr_s).reshape(b_s, s_s * N_HEADS, QK_ROPE_DIM)
    per_step = _pick_tile(n_pages, (32, 16, 8, 4, 2))
    chains = _pick_tile(per_step, (4, 2))
    a_s = _sample_attention(page_table, ql_b, qr_b, ckv_sb, kr_sb, pw["w_v"], attn_grp_norm[l][None],
                            cache_kv_latent[l], jnp.swapaxes(cache_k_rope[l], 1, 2),
                            pages=per_step // chains, chains=chains)
    a_st = a_s.transpose(1, 0, 2).reshape(n_s, ATTN_WIDTH)
    full_t = jnp.concatenate([state_conv[l].transpose(1, 0, 2), glu_s.reshape(s_s, b_s, CONV_CH)], axis=0)
    h2_s = _mix_sample(h_s, a_st, full_t, mw, s_new=s_s)
    y_s = _ffn(h2_s, *f2, fin, tm=n_s, final=True).reshape(s_s, b_s, d).transpose(1, 0, 2)

    bcast = lambda x: jnp.broadcast_to(x[None], (b_p,) + x.shape)
    new_ckv_p = jnp.concatenate([bcast(ckv_m), ckv_p.reshape(b_p, s_p, KV_RANK)], axis=1)[None]
    new_kr_p = jnp.concatenate([bcast(kr_m), kr_p.reshape(b_p, s_p, QK_ROPE_DIM)], axis=1)[None]
    n_keep = CONV_WIDTH - 1
    new_conv_p = glu_p.reshape(b_p, s_p, CONV_CH)[:, s_p - n_keep:][None]
    new_conv_s = jnp.concatenate([state_conv[l], glu_sb], axis=1)[:, -n_keep:][None]
    return (y_p, y_s, new_ckv_p, new_kr_p, new_conv_p, ckv_sb[None], kr_sb[None], new_conv_s)
```

```python
import functools
import math

import jax
import jax.numpy as jnp
from jax import lax
from jax.experimental import pallas as pl
from jax.experimental.pallas import tpu as pltpu

N_META = 16
N_HEADS = 8
QK_NOPE_DIM = 64
QK_ROPE_DIM = 32
V_HEAD_DIM = 64
Q_RANK = 256
KV_RANK = 256
CONV_CH = 512
CONV_WIDTH = 31
ROPE_THETA = 10000.0
EPS = 1e-6
PAGE_SIZE = 128
Q_HEAD_DIM = QK_NOPE_DIM + QK_ROPE_DIM
ATTN_WIDTH = N_HEADS * V_HEAD_DIM
SOFTMAX_SCALE = Q_HEAD_DIM ** -0.5
SCORE_SCALE = SOFTMAX_SCALE * math.log2(math.e)

LANES = 128
SUBLANES = 8
HEAD_SLOT = LANES
HALO = 32
CONV_ROWS = 64
HEAD_LOOKAHEAD = 3
FFN_CHUNK_ELEMS = 512 * 1408
PAGE_RING = 3
VT_ONES = 16
VT_ROWS = V_HEAD_DIM + VT_ONES
VMEM_LIMIT = 56 * 1024 * 1024

_F32 = jnp.float32
_BF16 = jnp.bfloat16


def _const_spec(shape):
    nd = len(shape)
    return pl.BlockSpec(shape, lambda *_: (0,) * nd, pipeline_mode=pl.Buffered(1))


def _rms(x, g):
    return x * lax.rsqrt(jnp.mean(x * x, axis=-1, keepdims=True) + EPS) * g


def _sigmoid(x):
    return 1.0 / (1.0 + jnp.exp(-x))


def _dot(a, b):
    return jnp.dot(a, b, preferred_element_type=_F32)


def _dot_nt(a, b):
    return lax.dot_general(a, b, (((1,), (1,)), ((), ())), preferred_element_type=_F32)


def _ffn_kernel(x_ref, g_ref, wg_ref, wu_ref, wd_ref, fg_ref, o_ref, *, ff_chunks, final):
    x = x_ref[...]
    xn = _rms(x, g_ref[...]).astype(_BF16)
    acc = jnp.zeros_like(x)
    for c0, cw in ff_chunks:
        gate = _dot(xn, wg_ref[:, c0:c0 + cw])
        up = _dot(xn, wu_ref[:, c0:c0 + cw])
        act = (gate * _sigmoid(gate) * up).astype(_BF16)
        acc = acc + _dot(act, wd_ref[c0:c0 + cw, :])
    y = x + 0.5 * acc
    if final:
        y = _rms(y, fg_ref[...])
    o_ref[...] = y


def _ffn(x, norm_g, wg, wu, wd, final_g, *, tm, final):
    t, d = x.shape
    d_ff = wg.shape[1]
    n_lane_tiles = d_ff // LANES
    n_chunks = max(1, min(n_lane_tiles, -(-tm * d_ff // FFN_CHUNK_ELEMS)))
    bounds = [(n_lane_tiles * c // n_chunks) * LANES for c in range(n_chunks + 1)]
    ff_chunks = tuple((bounds[c], bounds[c + 1] - bounds[c]) for c in range(n_chunks))
    return pl.pallas_call(
        functools.partial(_ffn_kernel, ff_chunks=ff_chunks, final=final),
        grid=(t // tm,),
        in_specs=[
            pl.BlockSpec((tm, d), lambda i: (i, 0)),
            _const_spec((1, d)),
            _const_spec((d, d_ff)),
            _const_spec((d, d_ff)),
            _const_spec((d_ff, d)),
            _const_spec((1, d)),
        ],
        out_specs=pl.BlockSpec((tm, d), lambda i: (i, 0)),
        out_shape=jax.ShapeDtypeStruct((t, d), _F32),
        compiler_params=pltpu.CompilerParams(
            dimension_semantics=("arbitrary",), vmem_limit_bytes=VMEM_LIMIT),
        name="ffn_final" if final else "ffn",
    )(x, norm_g, wg, wu, wd, final_g)


def _proj_kernel(h_ref, cosq_ref, sinq_ref, cosk_ref, sink_ref,
                 mixg_ref, win_ref, qg_ref, wqa_ref, wqb_ref, kvg_ref, *rest, absorbed):
    if absorbed:
        wabs_ref, wqr_ref, ckv_ref, kr_ref, glu_ref, ql_ref, qr_ref = rest
    else:
        wk1_ref, wvt_ref, vtb_ref, ckv_ref, kr_ref, glu_ref, q_ref, k_ref, vt_ref = rest
    n = _rms(h_ref[...], mixg_ref[...]).astype(_BF16)
    z = _dot(n, win_ref[...])
    o_kv = Q_RANK
    o_ca = o_kv + KV_RANK
    o_cb = o_ca + CONV_CH
    o_ka = o_cb + CONV_CH
    o_kb = o_ka + LANES

    glu_ref[...] = z[:, o_ca:o_cb] * _sigmoid(z[:, o_cb:o_ka])

    c = _rms(z[:, o_kv:o_ca], kvg_ref[...])
    ckv_ref[...] = c
    c_bf = c.astype(_BF16)
    kr = z[:, o_ka:o_kb] * cosk_ref[...] + z[:, o_kb:o_kb + LANES] * sink_ref[...]
    kr_ref[...] = kr[:, :QK_ROPE_DIM]

    qn = _rms(z[:, :Q_RANK], qg_ref[...]).astype(_BF16)
    qa = _dot(qn, wqa_ref[...])
    qb = _dot(qn, wqb_ref[...])
    cosq = cosq_ref[...]
    sinq = sinq_ref[...]
    q_heads = []
    for h in range(N_HEADS):
        sl = slice(h * HEAD_SLOT, (h + 1) * HEAD_SLOT)
        q_heads.append((qa[:, sl] * cosq + qb[:, sl] * sinq).astype(_BF16))
    q = jnp.concatenate(q_heads, axis=-1)
    if absorbed:
        ql_ref[...] = _dot(q, wabs_ref[...]).astype(_BF16)
        qr_ref[...] = _dot(q, wqr_ref[...]).astype(_BF16)
    else:
        q_ref[...] = q
        k_nope = _dot(c_bf, wk1_ref[...])
        kr_slot = pltpu.roll(kr, QK_NOPE_DIM, 1)
        k_ref[...] = jnp.concatenate(
            [(k_nope[:, h * HEAD_SLOT:(h + 1) * HEAD_SLOT] + kr_slot).astype(_BF16) for h in range(N_HEADS)],
            axis=-1)
        vt = (_dot_nt(wvt_ref[...], c_bf) + vtb_ref[...]).astype(_BF16)
        tkv = vt_ref.shape[2]
        for blk in range(vt_ref.shape[0]):
            vt_ref[blk] = vt[:, blk * tkv:(blk + 1) * tkv]


def _proj(h, tables, n_table_blocks, pw, *, tm, absorbed, tkv=None):
    t, d = h.shape
    cosq, sinq, cosk, sink = tables
    row = lambda i: (i, 0)
    tab = lambda i: (i % n_table_blocks, 0)
    in_specs = [
        pl.BlockSpec((tm, d), row),
        pl.BlockSpec((tm, LANES), tab), pl.BlockSpec((tm, LANES), tab),
        pl.BlockSpec((tm, LANES), tab), pl.BlockSpec((tm, LANES), tab),
        _const_spec(pw["mix_norm"].shape), _const_spec(pw["w_in"].shape),
        _const_spec(pw["q_norm"].shape), _const_spec(pw["w_qa"].shape),
        _const_spec(pw["w_qb"].shape), _const_spec(pw["kv_norm"].shape),
    ]
    args = [h, cosq, sinq, cosk, sink, pw["mix_norm"], pw["w_in"], pw["q_norm"],
            pw["w_qa"], pw["w_qb"], pw["kv_norm"]]
    out_shape = [jax.ShapeDtypeStruct((t, KV_RANK), _F32),
                 jax.ShapeDtypeStruct((t, QK_ROPE_DIM), _F32),
                 jax.ShapeDtypeStruct((t, CONV_CH), _F32)]
    out_specs = [pl.BlockSpec((tm, KV_RANK), row), pl.BlockSpec((tm, QK_ROPE_DIM), row),
                 pl.BlockSpec((tm, CONV_CH), row)]
    if absorbed:
        extra = ["w_abs", "w_qr"]
        widths = [N_HEADS * KV_RANK, N_HEADS * QK_ROPE_DIM]
    else:
        extra = ["w_k1", "w_vt", "vt_bias"]
        widths = [N_HEADS * HEAD_SLOT, N_HEADS * HEAD_SLOT]
    for name in extra:
        in_specs.append(_const_spec(pw[name].shape))
        args.append(pw[name])
    for w in widths:
        out_shape.append(jax.ShapeDtypeStruct((t, w), _BF16))
        out_specs.append(pl.BlockSpec((tm, w), row))
    if not absorbed:
        vr = N_HEADS * VT_ROWS
        out_shape.append(jax.ShapeDtypeStruct((t // tkv, vr, tkv), _BF16))
        out_specs.append(pl.BlockSpec((tm // tkv, vr, tkv), lambda i: (i, 0, 0)))
    return pl.pallas_call(
        functools.partial(_proj_kernel, absorbed=absorbed),
        grid=(t // tm,),
        in_specs=in_specs,
        out_specs=out_specs,
        out_shape=out_shape,
        compiler_params=pltpu.CompilerParams(
            dimension_semantics=("arbitrary",), vmem_limit_bytes=VMEM_LIMIT),
        name="proj_absorbed" if absorbed else "proj",
    )(*args)


def _prompt_attn_kernel(q_ref, k_ref, vt_ref, km_ref, vtm_ref, g_ref, o_ref, m_sc, acc_sc, *, tq, tk):
    i = pl.program_id(1)
    n_full = i * (tq // tk)

    def q_slot(h):
        return slice(h * HEAD_SLOT, (h + 1) * HEAD_SLOT)

    def v_rows(h):
        return slice(h * VT_ROWS, (h + 1) * VT_ROWS)

    def update(h, s, vt_t, c0, mask, with_meta):
        rs = v_rows(h)
        if mask is not None:
            s_diag = jnp.where(mask, s[:, :tk], -jnp.inf)
            s = s_diag if s.shape[1] == tk else jnp.concatenate([s_diag, s[:, tk:]], axis=1)
        s_max = jnp.max(s, axis=0, keepdims=True)
        if with_meta:
            p = jnp.exp2(s - s_max).astype(_BF16)
            acc_sc[rs, :] = _dot(vtm_ref[rs, :], p[:N_META]) + _dot(vt_t, p[N_META:])
            m_sc[h] = s_max
        else:
            m_old = m_sc[h, :, c0:]
            m_new = jnp.maximum(m_old, s_max)
            alpha = jnp.exp2(m_old - m_new)
            p = jnp.exp2(s - m_new)
            acc_sc[rs, c0:] = acc_sc[rs, c0:] * alpha + _dot(vt_t, p.astype(_BF16))
            m_sc[h, :, c0:] = m_new

    def tile(j, c0=0, mask=None, with_meta=False):
        ks = pl.ds(pl.multiple_of(j * tk, tk), tk)

        def scores(h):
            k_t = k_ref[ks, q_slot(h)]
            if with_meta:
                k_t = jnp.concatenate([km_ref[:, q_slot(h)], k_t], axis=0)
            return _dot_nt(k_t, q_ref[c0:, q_slot(h)])

        pending = [scores(h) for h in range(HEAD_LOOKAHEAD)]
        for h in range(N_HEADS):
            s = pending.pop(0)
            if h + HEAD_LOOKAHEAD < N_HEADS:
                pending.append(scores(h + HEAD_LOOKAHEAD))
            update(h, s, vt_ref[j, v_rows(h), :], c0, mask, with_meta)

    for d in range(tq // tk):
        c0 = d * tk
        n_meta = N_META if d == 0 else 0
        key_r = lax.broadcasted_iota(jnp.int32, (n_meta + tk, tk), 0)
        qry_c = lax.broadcasted_iota(jnp.int32, (n_meta + tk, tk), 1)
        tile(n_full + d, c0, mask=key_r - n_meta <= qry_c, with_meta=d == 0)

    per_iter = tq // tk

    def full_tiles(jj, carry):
        for u in range(per_iter):
            tile(jj * per_iter + u)
        return carry

    lax.fori_loop(0, i, full_tiles, 0)

    heads = []
    for h in range(N_HEADS):
        r0 = h * VT_ROWS
        inv_l = 1.0 / acc_sc[r0 + V_HEAD_DIM:r0 + V_HEAD_DIM + 1, :]
        heads.append(acc_sc[r0:r0 + V_HEAD_DIM, :] * inv_l)
    a_t = jnp.concatenate(heads, axis=0)
    ms = jnp.mean(a_t * a_t, axis=0, keepdims=True)
    a_n = a_t * lax.rsqrt(ms + EPS) * g_ref[...]
    o_ref[...] = a_n.T.astype(_BF16)


def _prompt_attention(q, k, vt, k_meta, vt_meta, grp_g_col, *, batch, seq, tq, tk):
    nq = seq // tq
    nk = seq // tk
    kw = N_HEADS * HEAD_SLOT
    vr = N_HEADS * VT_ROWS
    return pl.pallas_call(
        functools.partial(_prompt_attn_kernel, tq=tq, tk=tk),
        grid=(batch, nq),
        in_specs=[
            pl.BlockSpec((tq, kw), lambda b, i: (b * nq + i, 0)),
            pl.BlockSpec((seq, kw), lambda b, i: (b, 0)),
            pl.BlockSpec((nk, vr, tk), lambda b, i: (b, 0, 0)),
            _const_spec(k_meta.shape),
            _const_spec(vt_meta.shape),
            _const_spec((ATTN_WIDTH, 1)),
        ],
        out_specs=pl.BlockSpec((tq, ATTN_WIDTH), lambda b, i: (b * nq + i, 0)),
        out_shape=jax.ShapeDtypeStruct((batch * seq, ATTN_WIDTH), _BF16),
        scratch_shapes=[
            pltpu.VMEM((N_HEADS, 1, tq), _F32),
            pltpu.VMEM((vr, tq), _F32),
        ],
        compiler_params=pltpu.CompilerParams(
            dimension_semantics=("arbitrary", "arbitrary"), vmem_limit_bytes=VMEM_LIMIT),
        name="prompt_attn",
    )(q, k, vt, k_meta, vt_meta, grp_g_col)


def _sample_attn_kernel(pt_ref, ql_ref, qr_ref, cn_ref, kn_ref, wv_ref, g_ref, ckv_hbm, krt_hbm,
                        o_ref, cbuf, kbuf, sem, m_sc, l_sc, acc_sc, *, pages, chains, n_steps, s_new):
    b = pl.program_id(0)
    j = pl.program_id(1)
    nb = pl.num_programs(0)
    step = b * n_steps + j
    n_slots = cbuf.shape[0]
    slot = step % n_slots
    rows = s_new * N_HEADS
    per_step = pages * chains

    def page_copy(page, p, sl):
        return (pltpu.make_async_copy(ckv_hbm.at[page], cbuf.at[sl, p], sem.at[0, sl]),
                pltpu.make_async_copy(krt_hbm.at[page], kbuf.at[sl, p], sem.at[1, sl]))

    def start_pages(st):
        bb, jj, sl = st // n_steps, st % n_steps, st % n_slots
        for p in range(per_step):
            for cp in page_copy(pt_ref[bb, jj * per_step + p], p, sl):
                cp.start()

    @pl.when(step == 0)
    def _():
        for st in range(n_slots - 1):
            start_pages(st)

    @pl.when(step + n_slots - 1 < nb * n_steps)
    def _():
        start_pages(step + n_slots - 1)

    @pl.when(j == 0)
    def _():
        m_sc[...] = jnp.full(m_sc.shape, -jnp.inf, _F32)
        l_sc[...] = jnp.zeros(l_sc.shape, _F32)
        acc_sc[...] = jnp.zeros(acc_sc.shape, _F32)

    for p in range(per_step):
        for cp in page_copy(0, p, slot):
            cp.wait()

    ql = ql_ref[0]
    qr = qr_ref[0]
    n_keys = pages * PAGE_SIZE
    latents, scores = [], []
    for c in range(chains):
        c_bf = cbuf[slot, c * pages:(c + 1) * pages].reshape(n_keys, KV_RANK).astype(_BF16)
        krt = jnp.concatenate([kbuf[slot, c * pages + p] for p in range(pages)], axis=1).astype(_BF16)
        latents.append(c_bf)
        scores.append(_dot_nt(ql, c_bf) + _dot(qr, krt))
    for c in range(chains):
        c_bf, s = latents[c], scores[c]
        m = m_sc[c]
        m_new = jnp.maximum(m, jnp.max(s, axis=-1, keepdims=True))
        alpha = jnp.exp2(m - m_new)
        p = jnp.exp2(s - m_new)
        l_sc[c] = l_sc[c] * alpha + jnp.sum(p, axis=-1, keepdims=True)
        acc_sc[c] = acc_sc[c] * alpha + _dot(p.astype(_BF16), c_bf)
        m_sc[c] = m_new

    @pl.when(j == n_steps - 1)
    def _():
        qlf = ql.astype(_F32)
        qrf = qr.astype(_F32)
        tok = lax.broadcasted_iota(jnp.int32, (rows, 1), 0) // N_HEADS
        cn = cn_ref[0].astype(_BF16).astype(_F32)
        kn = kn_ref[0].astype(_BF16).astype(_F32)
        s_cols = []
        for t in range(s_new):
            st = (jnp.sum(qlf * cn[t:t + 1, :], axis=-1, keepdims=True)
                  + jnp.sum(qrf * kn[t:t + 1, :], axis=-1, keepdims=True))
            s_cols.append(jnp.where(tok >= t, st, -jnp.inf))
        m_new = m_sc[0]
        for c in range(1, chains):
            m_new = jnp.maximum(m_new, m_sc[c])
        for st in s_cols:
            m_new = jnp.maximum(m_new, st)
        l = jnp.zeros((rows, 1), _F32)
        acc = jnp.zeros((rows, KV_RANK), _F32)
        for c in range(chains):
            alpha = jnp.exp2(m_sc[c] - m_new)
            l = l + l_sc[c] * alpha
            acc = acc + acc_sc[c] * alpha
        for t in range(s_new):
            pt = jnp.exp2(s_cols[t] - m_new)
            l = l + pt
            acc = acc + pt.astype(_BF16).astype(_F32) * cn[t:t + 1, :]
        o = (acc * (1.0 / l)).astype(_BF16)
        full = _dot(o, wv_ref[...])
        head = lax.broadcasted_iota(jnp.int32, (rows, ATTN_WIDTH), 0) % N_HEADS
        lane_head = lax.broadcasted_iota(jnp.int32, (rows, ATTN_WIDTH), 1) // V_HEAD_DIM
        full = jnp.where(head == lane_head, full, 0.0)
        a = jnp.concatenate(
            [jnp.sum(full[t * N_HEADS:(t + 1) * N_HEADS], axis=0, keepdims=True) for t in range(s_new)],
            axis=0)
        o_ref[0] = _rms(a, g_ref[...]).astype(_BF16)


def _sample_attention(page_table, ql, qr, c_new, k_new, w_v, grp_g, cache_ckv, cache_krt, *, pages, chains):
    nb, n_pages = page_table.shape
    per_step = pages * chains
    n_steps = n_pages // per_step
    s_new = c_new.shape[1]
    rows = s_new * N_HEADS
    grid_spec = pltpu.PrefetchScalarGridSpec(
        num_scalar_prefetch=1,
        grid=(nb, n_steps),
        in_specs=[
            pl.BlockSpec((1, rows, KV_RANK), lambda b, j, pt: (b, 0, 0)),
            pl.BlockSpec((1, rows, QK_ROPE_DIM), lambda b, j, pt: (b, 0, 0)),
            pl.BlockSpec((1, s_new, KV_RANK), lambda b, j, pt: (b, 0, 0)),
            pl.BlockSpec((1, s_new, QK_ROPE_DIM), lambda b, j, pt: (b, 0, 0)),
            _const_spec((KV_RANK, ATTN_WIDTH)),
            _const_spec((1, ATTN_WIDTH)),
            pl.BlockSpec(memory_space=pl.ANY),
            pl.BlockSpec(memory_space=pl.ANY),
        ],
        out_specs=pl.BlockSpec((1, s_new, ATTN_WIDTH), lambda b, j, pt: (b, 0, 0)),
        scratch_shapes=[
            pltpu.VMEM((PAGE_RING, per_step, PAGE_SIZE, KV_RANK), _F32),
            pltpu.VMEM((PAGE_RING, per_step, QK_ROPE_DIM, PAGE_SIZE), _F32),
            pltpu.SemaphoreType.DMA((2, PAGE_RING)),
            pltpu.VMEM((chains, rows, 1), _F32),
            pltpu.VMEM((chains, rows, 1), _F32),
            pltpu.VMEM((chains, rows, KV_RANK), _F32),
        ],
    )
    return pl.pallas_call(
        functools.partial(_sample_attn_kernel, pages=pages, chains=chains, n_steps=n_steps, s_new=s_new),
        grid_spec=grid_spec,
        out_shape=jax.ShapeDtypeStruct((nb, s_new, ATTN_WIDTH), _BF16),
        compiler_params=pltpu.CompilerParams(
            dimension_semantics=("arbitrary", "arbitrary"), vmem_limit_bytes=VMEM_LIMIT),
        name="sample_attn",
    )(page_table, ql, qr, c_new, k_new, w_v, grp_g, cache_ckv, cache_krt)


def _mix_tail(y, h, a_n, cb_ref, lng_ref, lnb_ref, cg_ref, woa_ref, woc_ref):
    y = y + cb_ref[...]
    mu = jnp.mean(y, axis=-1, keepdims=True)
    yc = y - mu
    var = jnp.mean(yc * yc, axis=-1, keepdims=True)
    y = yc * lax.rsqrt(var + EPS) * lng_ref[...] + lnb_ref[...]
    y = y * _sigmoid(y)
    cv = _rms(y, cg_ref[...]).astype(_BF16)
    return h + _dot(a_n, woa_ref[...]) + _dot(cv, woc_ref[...])


def _shift_copies(ext_sc, sh_sc):
    n = sh_sc.shape[1]
    for r in range(1, SUBLANES):
        sh_sc[r - 1] = ext_sc[r:r + n, :]


def _conv_rows(ext_sc, sh_sc, cw_ref, r0):
    off = HALO - (CONV_WIDTH - 1)
    tiles = []
    for c0 in range(0, CONV_CH, LANES):
        cs = slice(c0, c0 + LANES)
        y = jnp.zeros((CONV_ROWS, LANES), _F32)
        for k in range(CONV_WIDTH):
            res = (off + k) % SUBLANES
            rows = slice(r0 + off + k - res, r0 + off + k - res + CONV_ROWS)
            src = ext_sc[rows, cs] if res == 0 else sh_sc[res - 1, rows, cs]
            y = y + src * cw_ref[k:k + 1, cs]
        tiles.append(y)
    return jnp.concatenate(tiles, axis=-1)


def _mix_prompt_kernel(h_ref, a_ref, glu_ref, prev_ref, first_ref, cw_ref, cb_ref, lng_ref, lnb_ref,
                       cg_ref, woa_ref, woc_ref, o_ref, ext_sc, sh_sc, y_sc, *, tm):
    i = pl.program_id(1)
    ext_sc[:HALO, :] = jnp.where(i == 0, first_ref[...], prev_ref[...])
    ext_sc[HALO:, :] = glu_ref[...]
    _shift_copies(ext_sc, sh_sc)
    for r0 in range(0, tm, CONV_ROWS):
        y_sc[r0:r0 + CONV_ROWS, :] = _conv_rows(ext_sc, sh_sc, cw_ref, r0)
    o_ref[...] = _mix_tail(y_sc[...], h_ref[...], a_ref[...], cb_ref, lng_ref, lnb_ref, cg_ref,
                           woa_ref, woc_ref)


def _mix_prompt(h, a_n, glu, first_halo, mw, *, batch, seq, tm):
    nt = seq // tm
    d = h.shape[1]
    hb = tm // HALO
    row = lambda b, i: (b * nt + i, 0)
    return pl.pallas_call(
        functools.partial(_mix_prompt_kernel, tm=tm),
        grid=(batch, nt),
        in_specs=[
            pl.BlockSpec((tm, d), row),
            pl.BlockSpec((tm, ATTN_WIDTH), row),
            pl.BlockSpec((tm, CONV_CH), row),
            pl.BlockSpec((HALO, CONV_CH), lambda b, i: (jnp.maximum((b * nt + i) * hb - 1, 0), 0)),
            _const_spec((HALO, CONV_CH)),
            _const_spec(mw["conv_w"].shape), _const_spec(mw["conv_b"].shape),
            _const_spec(mw["ln_g"].shape), _const_spec(mw["ln_b"].shape),
            _const_spec(mw["conv_grp"].shape),
            _const_spec(mw["w_out_a"].shape), _const_spec(mw["w_out_c"].shape),
        ],
        out_specs=pl.BlockSpec((tm, d), row),
        out_shape=jax.ShapeDtypeStruct(h.shape, _F32),
        scratch_shapes=[pltpu.VMEM((HALO + tm, CONV_CH), _F32),
                        pltpu.VMEM((SUBLANES - 1, HALO + tm - SUBLANES, CONV_CH), _F32),
                        pltpu.VMEM((tm, CONV_CH), _F32)],
        compiler_params=pltpu.CompilerParams(
            dimension_semantics=("arbitrary", "arbitrary"), vmem_limit_bytes=VMEM_LIMIT),
        name="mix_prompt",
    )(h, a_n, glu, glu, first_halo, mw["conv_w"], mw["conv_b"], mw["ln_g"], mw["ln_b"],
      mw["conv_grp"], mw["w_out_a"], mw["w_out_c"])


def _mix_sample_kernel(h_ref, a_ref, full_ref, cw_ref, cb_ref, lng_ref, lnb_ref,
                       cg_ref, woa_ref, woc_ref, o_ref, *, s_new):
    ys = []
    for t in range(s_new):
        y = jnp.zeros(full_ref.shape[1:], _F32)
        for k in range(CONV_WIDTH):
            y = y + full_ref[t + k] * cw_ref[k:k + 1, :]
        ys.append(y)
    y = jnp.concatenate(ys, axis=0)
    o_ref[...] = _mix_tail(y, h_ref[...], a_ref[...], cb_ref, lng_ref, lnb_ref, cg_ref, woa_ref, woc_ref)


def _mix_sample(h, a_n, full_t, mw, *, s_new):
    t, d = h.shape
    return pl.pallas_call(
        functools.partial(_mix_sample_kernel, s_new=s_new),
        grid=(1,),
        in_specs=[
            _const_spec((t, d)), _const_spec((t, ATTN_WIDTH)), _const_spec(full_t.shape),
            _const_spec(mw["conv_w"].shape), _const_spec(mw["conv_b"].shape),
            _const_spec(mw["ln_g"].shape), _const_spec(mw["ln_b"].shape),
            _const_spec(mw["conv_grp"].shape),
            _const_spec(mw["w_out_a"].shape), _const_spec(mw["w_out_c"].shape),
        ],
        out_specs=pl.BlockSpec((t, d), lambda i: (0, 0)),
        out_shape=jax.ShapeDtypeStruct((t, d), _F32),
        compiler_params=pltpu.CompilerParams(
            dimension_semantics=("arbitrary",), vmem_limit_bytes=VMEM_LIMIT),
        name="mix_sample",
    )(h, a_n, full_t, mw["conv_w"], mw["conv_b"], mw["ln_g"], mw["ln_b"],
      mw["conv_grp"], mw["w_out_a"], mw["w_out_c"])


def _rope_tables(pos):
    half = QK_ROPE_DIM // 2
    inv = ROPE_THETA ** (-jnp.arange(half, dtype=_F32) / half)
    ang = pos.astype(_F32)[:, None] * inv[None, :]
    cos = jnp.cos(ang)
    sin = jnp.sin(ang)
    n = pos.shape[0]
    cos2 = jnp.concatenate([cos, cos], axis=-1)
    sin2 = jnp.concatenate([sin, sin], axis=-1)
    pad_q = jnp.zeros((n, HEAD_SLOT - Q_HEAD_DIM), _F32)
    cosq = SCORE_SCALE * jnp.concatenate([jnp.ones((n, QK_NOPE_DIM), _F32), cos2, pad_q], axis=-1)
    sinq = SCORE_SCALE * jnp.concatenate([jnp.zeros((n, QK_NOPE_DIM), _F32), sin2, pad_q], axis=-1)
    pad_k = jnp.zeros((n, LANES - QK_ROPE_DIM), _F32)
    cosk = jnp.concatenate([cos2, pad_k], axis=-1)
    sink = jnp.concatenate([sin2, pad_k], axis=-1)
    return cosq, sinq, cosk, sink


def _rot_cols(w):
    half = QK_ROPE_DIM // 2
    return jnp.concatenate([-w[..., half:], w[..., :half]], axis=-1)


def _layer_weights(l, mix_norm, w_in, q_norm, w_uq, kv_norm, w_uk, w_uv):
    d = w_in.shape[1]
    win = w_in[l]
    o_kv = Q_RANK
    o_kr = o_kv + KV_RANK
    o_cv = o_kr + QK_ROPE_DIM
    w_kr = win[:, o_kr:o_cv]
    zpad = jnp.zeros((d, LANES - QK_ROPE_DIM), _F32)
    win_p = jnp.concatenate([win[:, :o_kr], win[:, o_cv:], w_kr, zpad, _rot_cols(w_kr), zpad], axis=-1)

    wq = w_uq[l].reshape(Q_RANK, N_HEADS, Q_HEAD_DIM)
    zq = jnp.zeros((Q_RANK, N_HEADS, HEAD_SLOT - Q_HEAD_DIM), _F32)
    w_qa = jnp.concatenate([wq, zq], axis=-1).reshape(Q_RANK, N_HEADS * HEAD_SLOT)
    w_qb = jnp.concatenate([jnp.zeros((Q_RANK, N_HEADS, QK_NOPE_DIM), _F32),
                            _rot_cols(wq[..., QK_NOPE_DIM:]), zq], axis=-1).reshape(Q_RANK, N_HEADS * HEAD_SLOT)

    wuk = w_uk[l]
    zk = jnp.zeros((KV_RANK, N_HEADS, HEAD_SLOT - QK_NOPE_DIM), _F32)
    w_k1 = jnp.concatenate([wuk, zk], axis=-1).reshape(KV_RANK, N_HEADS * HEAD_SLOT)
    eye = jnp.eye(QK_ROPE_DIM, dtype=_F32)
    place = jnp.concatenate([jnp.zeros((QK_ROPE_DIM, QK_NOPE_DIM), _F32), eye,
                             jnp.zeros((QK_ROPE_DIM, HEAD_SLOT - Q_HEAD_DIM), _F32)], axis=-1)
    w_v =w_uv[l].reshape(KV_RANK, ATTN_WIDTH)

    w_vt = jnp.concatenate([w_uv[l].transpose(1, 2, 0), jnp.zeros((N_HEADS, VT_ONES, KV_RANK), _F32)],
                           axis=1).reshape(N_HEADS * VT_ROWS, KV_RANK)
    vt_bias = jnp.tile(jnp.concatenate([jnp.zeros((V_HEAD_DIM, 1), _F32), jnp.ones((VT_ONES, 1), _F32)],
                                       axis=0), (N_HEADS, 1))

    head_eye = jnp.eye(N_HEADS, dtype=_F32)[:, None, :, None]
    slot_abs = jnp.concatenate([wuk.transpose(1, 2, 0),
                                jnp.zeros((N_HEADS, HEAD_SLOT - QK_NOPE_DIM, KV_RANK), _F32)], axis=1)
    w_abs = (head_eye * slot_abs[:, :, None, :]).reshape(N_HEADS * HEAD_SLOT, N_HEADS * KV_RANK)
    w_qr = (head_eye * place.T[None, :, None, :]).reshape(N_HEADS * HEAD_SLOT, N_HEADS * QK_ROPE_DIM)

    bf = lambda x: x.astype(_BF16)
    return {
        "mix_norm": mix_norm[l][None], "w_in": bf(win_p), "q_norm": q_norm[l][None],
        "w_qa": bf(w_qa), "w_qb": bf(w_qb), "kv_norm": kv_norm[l][None],
        "w_k1": bf(w_k1), "w_v": bf(w_v), "w_vt": bf(w_vt), "vt_bias": vt_bias,
        "w_abs": bf(w_abs), "w_qr": bf(w_qr),
    }


def _pick_tile(n, prefer):
    for t in prefer:
        if n % t == 0:
            return t
    return n


def kernel(x_prompt, x_sample, cache_kv_latent, cache_k_rope, state_conv, page_table, meta_tokens,
           ffn1_norm, ffn1_w_gate, ffn1_w_up, ffn1_w_down, mix_norm, w_in, q_norm, w_uq, kv_norm,
           w_uk, w_uv, conv_w, conv_b, conv_ln_g, conv_ln_b, attn_grp_norm, conv_grp_norm, w_out,
           ffn2_norm, ffn2_w_gate, ffn2_w_up, ffn2_w_down, final_norm):
    b_p, s_p, d = x_prompt.shape
    b_s, s_s, _ = x_sample.shape
    depth = ffn1_norm.shape[0]
    assert depth == 1, "the meta rows' later-layer inputs are not carried; single layer only"
    n_pages = page_table.shape[1]
    past_len = n_pages * PAGE_SIZE
    assert CONV_WIDTH - 1 <= HALO and N_META <= HALO
    l = 0
    bf = lambda x: x.astype(_BF16)

    pw = _layer_weights(l, mix_norm, w_in, q_norm, w_uq, kv_norm, w_uk, w_uv)
    f1 = (ffn1_norm[l][None], bf(ffn1_w_gate[l]), bf(ffn1_w_up[l]), bf(ffn1_w_down[l]))
    f2 = (ffn2_norm[l][None], bf(ffn2_w_gate[l]), bf(ffn2_w_up[l]), bf(ffn2_w_down[l]))
    fin = final_norm[None]
    mw = {
        "conv_w": jnp.concatenate([conv_w[l], jnp.zeros((1, CONV_CH), _F32)], axis=0),
        "conv_b": conv_b[l][None], "ln_g": conv_ln_g[l][None], "ln_b": conv_ln_b[l][None],
        "conv_grp": conv_grp_norm[l][None],
        "w_out_a": bf(w_out[l][:ATTN_WIDTH]), "w_out_c": bf(w_out[l][ATTN_WIDTH:]),
    }

    tm_p = _pick_tile(s_p, (512, 256, 128))
    tq = _pick_tile(s_p, (1024, 512, 256, 128))
    tk = _pick_tile(tq, (256, 128))
    assert tm_p % CONV_ROWS == 0 and tm_p % HALO == 0

    h_m = _ffn(meta_tokens, *f1, fin, tm=N_META, final=False)
    tab_m = _rope_tables(jnp.arange(N_META, dtype=jnp.int32))
    ckv_m, kr_m, glu_m, _, k_m, vt_m = _proj(h_m, tab_m, 1, pw, tm=N_META, absorbed=False, tkv=N_META)
    first_halo = jnp.concatenate([jnp.zeros((HALO - N_META, CONV_CH), _F32), glu_m], axis=0)

    xp = x_prompt.reshape(b_p * s_p, d)
    h_p = _ffn(xp, *f1, fin, tm=tm_p, final=False)
    tab_p = _rope_tables(N_META + jnp.arange(s_p, dtype=jnp.int32))
    ckv_p, kr_p, glu_p, q_p, k_p, vt_p = _proj(h_p, tab_p, s_p // tm_p, pw, tm=tm_p, absorbed=False, tkv=tk)
    a_p = _prompt_attention(q_p, k_p, vt_p, k_m, vt_m[0], attn_grp_norm[l][:, None],
                            batch=b_p, seq=s_p, tq=tq, tk=tk)
    h2_p = _mix_prompt(h_p, a_p, glu_p, first_halo, mw, batch=b_p, seq=s_p, tm=tm_p)
    y_p = _ffn(h2_p, *f2, fin, tm=tm_p, final=True).reshape(b_p, s_p, d)

    n_s = b_s * s_s
    xs = x_sample.transpose(1, 0, 2).reshape(n_s, d)
    h_s = _ffn(xs, *f1, fin, tm=n_s, final=False)
    pos_s = past_len + jnp.repeat(jnp.arange(s_s, dtype=jnp.int32), b_s)
    ckv_s, kr_s, glu_s, ql_s, qr_s = _proj(h_s, _rope_tables(pos_s), 1, pw, tm=n_s, absorbed=True)
    seq_major = lambda x: x.reshape(s_s, b_s, -1).transpose(1, 0, 2)
    ckv_sb, kr_sb, glu_sb = seq_major(ckv_s), seq_major(kr_s), seq_major(glu_s)
    ql_b = seq_major(ql_s).reshape(b_s, s_s * N_HEADS, KV_RANK)
    qr_b = seq_major(qr_s).reshape(b_s, s_s * N_HEADS, QK_ROPE_DIM)
    per_step = _pick_tile(n_pages, (32, 16, 8, 4, 2))
    chains = _pick_tile(per_step, (4, 2))
    a_s = _sample_attention(page_table, ql_b, qr_b, ckv_sb, kr_sb, pw["w_v"], attn_grp_norm[l][None],
                            cache_kv_latent[l], jnp.swapaxes(cache_k_rope[l], 1, 2),
                            pages=per_step // chains, chains=chains)
    a_st = a_s.transpose(1, 0, 2).reshape(n_s, ATTN_WIDTH)
    full_t = jnp.concatenate([state_conv[l].transpose(1, 0, 2), glu_s.reshape(s_s, b_s, CONV_CH)], axis=0)
    h2_s = _mix_sample(h_s, a_st, full_t, mw, s_new=s_s)
    y_s = _ffn(h2_s, *f2, fin, tm=n_s, final=True).reshape(s_s, b_s, d).transpose(1, 0, 2)

    bcast = lambda x: jnp.broadcast_to(x[None], (b_p,) + x.shape)
    new_ckv_p = jnp.concatenate([bcast(ckv_m), ckv_p.reshape(b_p, s_p, KV_RANK)], axis=1)[None]
    new_kr_p = jnp.concatenate([bcast(kr_m), kr_p.reshape(b_p, s_p, QK_ROPE_DIM)], axis=1)[None]
    n_keep = CONV_WIDTH - 1
    new_conv_p = glu_p.reshape(b_p, s_p, CONV_CH)[:, s_p - n_keep:][None]
    new_conv_s = jnp.concatenate([state_conv[l], glu_sb], axis=1)[:, -n_keep:][None]
    return (y_p, y_s, new_ckv_p, new_kr_p, new_conv_p, ckv_sb[None], kr_sb[None], new_conv_s)
```

```python
import functools
import math

import jax
import jax.numpy as jnp
from jax import lax
from jax.experimental import pallas as pl
from jax.experimental.pallas import tpu as pltpu

N_META = 16
N_HEADS = 8
QK_NOPE_DIM = 64
QK_ROPE_DIM = 32
V_HEAD_DIM = 64
Q_RANK = 256
KV_RANK = 256
CONV_CH = 512
CONV_WIDTH = 31
ROPE_THETA = 10000.0
EPS = 1e-6
PAGE_SIZE = 128
Q_HEAD_DIM = QK_NOPE_DIM + QK_ROPE_DIM
ATTN_WIDTH = N_HEADS * V_HEAD_DIM
SOFTMAX_SCALE = Q_HEAD_DIM ** -0.5
SCORE_SCALE = SOFTMAX_SCALE * math.log2(math.e)

LANES = 128
SUBLANES = 8
HEAD_SLOT = LANES
HALO = 32
CONV_ROWS = 64
HEAD_LOOKAHEAD = 3
FFN_CHUNK_ELEMS = 512 * 1408
PAGE_RING = 2
VT_ONES = 16
VT_ROWS = V_HEAD_DIM + VT_ONES
VMEM_LIMIT = 56 * 1024 * 1024

_F32 = jnp.float32
_BF16 = jnp.bfloat16


def _const_spec(shape):
    nd = len(shape)
    return pl.BlockSpec(shape, lambda *_: (0,) * nd, pipeline_mode=pl.Buffered(1))


def _rms(x, g):
    return x * lax.rsqrt(jnp.mean(x * x, axis=-1, keepdims=True) + EPS) * g


def _sigmoid(x):
    return 1.0 / (1.0 + jnp.exp(-x))


def _dot(a, b):
    return jnp.dot(a, b, preferred_element_type=_F32)


def _dot_nt(a, b):
    return lax.dot_general(a, b, (((1,), (1,)), ((), ())), preferred_element_type=_F32)


def _ffn_kernel(x_ref, g_ref, wg_ref, wu_ref, wd_ref, fg_ref, o_ref, *, ff_chunks, final):
    x = x_ref[...]
    xn = _rms(x, g_ref[...]).astype(_BF16)
    acc = jnp.zeros_like(x)
    for c0, cw in ff_chunks:
        gate = _dot(xn, wg_ref[:, c0:c0 + cw])
        up = _dot(xn, wu_ref[:, c0:c0 + cw])
        act = (gate * _sigmoid(gate) * up).astype(_BF16)
        acc = acc + _dot(act, wd_ref[c0:c0 + cw, :])
    y = x + 0.5 * acc
    if final:
        y = _rms(y, fg_ref[...])
    o_ref[...] = y


def _ffn(x, norm_g, wg, wu, wd, final_g, *, tm, final):
    t, d = x.shape
    d_ff = wg.shape[1]
    n_lane_tiles = d_ff // LANES
    n_chunks = max(1, min(n_lane_tiles, -(-tm * d_ff // FFN_CHUNK_ELEMS)))
    bounds = [(n_lane_tiles * c // n_chunks) * LANES for c in range(n_chunks + 1)]
    ff_chunks = tuple((bounds[c], bounds[c + 1] - bounds[c]) for c in range(n_chunks))
    return pl.pallas_call(
        functools.partial(_ffn_kernel, ff_chunks=ff_chunks, final=final),
        grid=(t // tm,),
        in_specs=[
            pl.BlockSpec((tm, d), lambda i: (i, 0)),
            _const_spec((1, d)),
            _const_spec((d, d_ff)),
            _const_spec((d, d_ff)),
            _const_spec((d_ff, d)),
            _const_spec((1, d)),
        ],
        out_specs=pl.BlockSpec((tm, d), lambda i: (i, 0)),
        out_shape=jax.ShapeDtypeStruct((t, d), _F32),
        compiler_params=pltpu.CompilerParams(
            dimension_semantics=("arbitrary",), vmem_limit_bytes=VMEM_LIMIT),
        name="ffn_final" if final else "ffn",
    )(x, norm_g, wg, wu, wd, final_g)


def _proj_kernel(h_ref, cosq_ref, sinq_ref, cosk_ref, sink_ref,
                 mixg_ref, win_ref, qg_ref, wqa_ref, wqb_ref, kvg_ref, *rest, absorbed):
    if absorbed:
        wabs_ref, wqr_ref, ckv_ref, kr_ref, glu_ref, ql_ref, qr_ref = rest
    else:
        wk1_ref, wvt_ref, vtb_ref, ckv_ref, kr_ref, glu_ref, q_ref, k_ref, vt_ref = rest
    n = _rms(h_ref[...], mixg_ref[...]).astype(_BF16)
    z = _dot(n, win_ref[...])
    o_kv = Q_RANK
    o_ca = o_kv + KV_RANK
    o_cb = o_ca + CONV_CH
    o_ka = o_cb + CONV_CH
    o_kb = o_ka + LANES

    glu_ref[...] = z[:, o_ca:o_cb] * _sigmoid(z[:, o_cb:o_ka])

    c = _rms(z[:, o_kv:o_ca], kvg_ref[...])
    ckv_ref[...] = c
    c_bf = c.astype(_BF16)
    kr = z[:, o_ka:o_kb] * cosk_ref[...] + z[:, o_kb:o_kb + LANES] * sink_ref[...]
    kr_ref[...] = kr[:, :QK_ROPE_DIM]

    qn = _rms(z[:, :Q_RANK], qg_ref[...]).astype(_BF16)
    qa = _dot(qn, wqa_ref[...])
    qb = _dot(qn, wqb_ref[...])
    cosq = cosq_ref[...]
    sinq = sinq_ref[...]
    q_heads = []
    for h in range(N_HEADS):
        sl = slice(h * HEAD_SLOT, (h + 1) * HEAD_SLOT)
        q_heads.append((qa[:, sl] * cosq + qb[:, sl] * sinq).astype(_BF16))
    q = jnp.concatenate(q_heads, axis=-1)
    if absorbed:
        ql_ref[...] = _dot(q, wabs_ref[...]).astype(_BF16)
        qr_ref[...] = _dot(q, wqr_ref[...]).astype(_BF16)
    else:
        q_ref[...] = q
        k_nope = _dot(c_bf, wk1_ref[...])
        kr_slot = pltpu.roll(kr, QK_NOPE_DIM, 1)
        k_ref[...] = jnp.concatenate(
            [(k_nope[:, h * HEAD_SLOT:(h + 1) * HEAD_SLOT] + kr_slot).astype(_BF16) for h in range(N_HEADS)],
            axis=-1)
        vt = (_dot_nt(wvt_ref[...], c_bf) + vtb_ref[...]).astype(_BF16)
        tkv = vt_ref.shape[2]
        for blk in range(vt_ref.shape[0]):
            vt_ref[blk] = vt[:, blk * tkv:(blk + 1) * tkv]


def _proj(h, tables, n_table_blocks, pw, *, tm, absorbed, tkv=None):
    t, d = h.shape
    cosq, sinq, cosk, sink = tables
    row = lambda i: (i, 0)
    tab = lambda i: (i % n_table_blocks, 0)
    in_specs = [
        pl.BlockSpec((tm, d), row),
        pl.BlockSpec((tm, LANES), tab), pl.BlockSpec((tm, LANES), tab),
        pl.BlockSpec((tm, LANES), tab), pl.BlockSpec((tm, LANES), tab),
        _const_spec(pw["mix_norm"].shape), _const_spec(pw["w_in"].shape),
        _const_spec(pw["q_norm"].shape), _const_spec(pw["w_qa"].shape),
        _const_spec(pw["w_qb"].shape), _const_spec(pw["kv_norm"].shape),
    ]
    args = [h, cosq, sinq, cosk, sink, pw["mix_norm"], pw["w_in"], pw["q_norm"],
            pw["w_qa"], pw["w_qb"], pw["kv_norm"]]
    out_shape = [jax.ShapeDtypeStruct((t, KV_RANK), _F32),
                 jax.ShapeDtypeStruct((t, QK_ROPE_DIM), _F32),
                 jax.ShapeDtypeStruct((t, CONV_CH), _F32)]
    out_specs = [pl.BlockSpec((tm, KV_RANK), row), pl.BlockSpec((tm, QK_ROPE_DIM), row),
                 pl.BlockSpec((tm, CONV_CH), row)]
    if absorbed:
        extra = ["w_abs", "w_qr"]
        widths = [N_HEADS * KV_RANK, N_HEADS * QK_ROPE_DIM]
    else:
        extra = ["w_k1", "w_vt", "vt_bias"]
        widths = [N_HEADS * HEAD_SLOT, N_HEADS * HEAD_SLOT]
    for name in extra:
        in_specs.append(_const_spec(pw[name].shape))
        args.append(pw[name])
    for w in widths:
        out_shape.append(jax.ShapeDtypeStruct((t, w), _BF16))
        out_specs.append(pl.BlockSpec((tm, w), row))
    if not absorbed:
        vr = N_HEADS * VT_ROWS
        out_shape.append(jax.ShapeDtypeStruct((t // tkv, vr, tkv), _BF16))
        out_specs.append(pl.BlockSpec((tm // tkv, vr, tkv), lambda i: (i, 0, 0)))
    return pl.pallas_call(
        functools.partial(_proj_kernel, absorbed=absorbed),
        grid=(t // tm,),
        in_specs=in_specs,
        out_specs=out_specs,
        out_shape=out_shape,
        compiler_params=pltpu.CompilerParams(
            dimension_semantics=("arbitrary",), vmem_limit_bytes=VMEM_LIMIT),
        name="proj_absorbed" if absorbed else "proj",
    )(*args)


def _prompt_attn_kernel(q_ref, k_ref, vt_ref, km_ref, vtm_ref, g_ref, o_ref, m_sc, acc_sc, *, tq, tk):
    i = pl.program_id(1)
    n_full = i * (tq // tk)

    def q_slot(h):
        return slice(h * HEAD_SLOT, (h + 1) * HEAD_SLOT)

    def v_rows(h):
        return slice(h * VT_ROWS, (h + 1) * VT_ROWS)

    def update(h, s, vt_t, c0, mask, with_meta):
        rs = v_rows(h)
        if mask is not None:
            s_diag = jnp.where(mask, s[:, :tk], -jnp.inf)
            s = s_diag if s.shape[1] == tk else jnp.concatenate([s_diag, s[:, tk:]], axis=1)
        s_max = jnp.max(s, axis=0, keepdims=True)
        if with_meta:
            p = jnp.exp2(s - s_max).astype(_BF16)
            acc_sc[rs, :] = _dot(vtm_ref[rs, :], p[:N_META]) + _dot(vt_t, p[N_META:])
            m_sc[h] = s_max
        else:
            m_old = m_sc[h, :, c0:]
            m_new = jnp.maximum(m_old, s_max)
            alpha = jnp.exp2(m_old - m_new)
            p = jnp.exp2(s - m_new)
            acc_sc[rs, c0:] = acc_sc[rs, c0:] * alpha + _dot(vt_t, p.astype(_BF16))
            m_sc[h, :, c0:] = m_new

    def tile(j, c0=0, mask=None, with_meta=False):
        ks = pl.ds(pl.multiple_of(j * tk, tk), tk)

        def scores(h):
            k_t = k_ref[ks, q_slot(h)]
            if with_meta:
                k_t = jnp.concatenate([km_ref[:, q_slot(h)], k_t], axis=0)
            return _dot_nt(k_t, q_ref[c0:, q_slot(h)])

        pending = [scores(h) for h in range(HEAD_LOOKAHEAD)]
        for h in range(N_HEADS):
            s = pending.pop(0)
            if h + HEAD_LOOKAHEAD < N_HEADS:
                pending.append(scores(h + HEAD_LOOKAHEAD))
            update(h, s, vt_ref[j, v_rows(h), :], c0, mask, with_meta)

    for d in range(tq // tk):
        c0 = d * tk
        n_meta = N_META if d == 0 else 0
        key_r = lax.broadcasted_iota(jnp.int32, (n_meta + tk, tk), 0)
        qry_c = lax.broadcasted_iota(jnp.int32, (n_meta + tk, tk), 1)
        tile(n_full + d, c0, mask=key_r - n_meta <= qry_c, with_meta=d == 0)

    per_iter = tq // tk

    def full_tiles(jj, carry):
        for u in range(per_iter):
            tile(jj * per_iter + u)
        return carry

    lax.fori_loop(0, i, full_tiles, 0)

    heads = []
    for h in range(N_HEADS):
        r0 = h * VT_ROWS
        inv_l = 1.0 / acc_sc[r0 + V_HEAD_DIM:r0 + V_HEAD_DIM + 1, :]
        heads.append(acc_sc[r0:r0 + V_HEAD_DIM, :] * inv_l)
    a_t = jnp.concatenate(heads, axis=0)
    ms = jnp.mean(a_t * a_t, axis=0, keepdims=True)
    a_n = a_t * lax.rsqrt(ms + EPS) * g_ref[...]
    o_ref[...] = a_n.T.astype(_BF16)


def _prompt_attention(q, k, vt, k_meta, vt_meta, grp_g_col, *, batch, seq, tq, tk):
    nq = seq // tq
    nk = seq // tk
    kw = N_HEADS * HEAD_SLOT
    vr = N_HEADS * VT_ROWS
    return pl.pallas_call(
        functools.partial(_prompt_attn_kernel, tq=tq, tk=tk),
        grid=(batch, nq),
        in_specs=[
            pl.BlockSpec((tq, kw), lambda b, i: (b * nq + i, 0)),
            pl.BlockSpec((seq, kw), lambda b, i: (b, 0)),
            pl.BlockSpec((nk, vr, tk), lambda b, i: (b, 0, 0)),
            _const_spec(k_meta.shape),
            _const_spec(vt_meta.shape),
            _const_spec((ATTN_WIDTH, 1)),
        ],
        out_specs=pl.BlockSpec((tq, ATTN_WIDTH), lambda b, i: (b * nq + i, 0)),
        out_shape=jax.ShapeDtypeStruct((batch * seq, ATTN_WIDTH), _BF16),
        scratch_shapes=[
            pltpu.VMEM((N_HEADS, 1, tq), _F32),
            pltpu.VMEM((vr, tq), _F32),
        ],
        compiler_params=pltpu.CompilerParams(
            dimension_semantics=("arbitrary", "arbitrary"), vmem_limit_bytes=VMEM_LIMIT),
        name="prompt_attn",
    )(q, k, vt, k_meta, vt_meta, grp_g_col)


def _sample_attn_kernel(pt_ref, ql_ref, qr_ref, cn_ref, kn_ref, wv_ref, g_ref, ckv_hbm, krt_hbm,
                        o_ref, cbuf, kbuf, sem, m_sc, l_sc, acc_sc, *, pages, chains, n_steps, s_new):
    b = pl.program_id(0)
    j = pl.program_id(1)
    nb = pl.num_programs(0)
    step = b * n_steps + j
    n_slots = cbuf.shape[0]
    slot = step % n_slots
    rows = s_new * N_HEADS
    per_step = pages * chains

    def page_copy(page, p, sl):
        return (pltpu.make_async_copy(ckv_hbm.at[page], cbuf.at[sl, p], sem.at[0, sl]),
                pltpu.make_async_copy(krt_hbm.at[page], kbuf.at[sl, p], sem.at[1, sl]))

    def start_pages(st):
        bb, jj, sl = st // n_steps, st % n_steps, st % n_slots
        for p in range(per_step):
            for cp in page_copy(pt_ref[bb, jj * per_step + p], p, sl):
                cp.start()

    @pl.when(step == 0)
    def _():
        for st in range(n_slots - 1):
            start_pages(st)

    @pl.when(step + n_slots - 1 < nb * n_steps)
    def _():
        start_pages(step + n_slots - 1)

    @pl.when(j == 0)
    def _():
        m_sc[...] = jnp.full(m_sc.shape, -jnp.inf, _F32)
        l_sc[...] = jnp.zeros(l_sc.shape, _F32)
        acc_sc[...] = jnp.zeros(acc_sc.shape, _F32)

    for p in range(per_step):
        for cp in page_copy(0, p, slot):
            cp.wait()

    ql = ql_ref[0]
    qr = qr_ref[0]
    n_keys = pages * PAGE_SIZE
    latents, scores = [], []
    for c in range(chains):
        c_bf = cbuf[slot, c * pages:(c + 1) * pages].reshape(n_keys, KV_RANK).astype(_BF16)
        krt = jnp.concatenate([kbuf[slot, c * pages + p] for p in range(pages)], axis=1).astype(_BF16)
        latents.append(c_bf)
        scores.append(_dot_nt(ql, c_bf) + _dot(qr, krt))
    for c in range(chains):
        c_bf, s = latents[c], scores[c]
        m = m_sc[c]
        m_new = jnp.maximum(m, jnp.max(s, axis=-1, keepdims=True))
        alpha = jnp.exp2(m - m_new)
        p = jnp.exp2(s - m_new)
        l_sc[c] = l_sc[c] * alpha + jnp.sum(p, axis=-1, keepdims=True)
        acc_sc[c] = acc_sc[c] * alpha + _dot(p.astype(_BF16), c_bf)
        m_sc[c] = m_new

    @pl.when(j == n_steps - 1)
    def _():
        qlf = ql.astype(_F32)
        qrf = qr.astype(_F32)
        tok = lax.broadcasted_iota(jnp.int32, (rows, 1), 0) // N_HEADS
        cn = cn_ref[0].astype(_BF16).astype(_F32)
        kn = kn_ref[0].astype(_BF16).astype(_F32)
        s_cols = []
        for t in range(s_new):
            st = (jnp.sum(qlf * cn[t:t + 1, :], axis=-1, keepdims=True)
                  + jnp.sum(qrf * kn[t:t + 1, :], axis=-1, keepdims=True))
            s_cols.append(jnp.where(tok >= t, st, -jnp.inf))
        m_new = m_sc[0]
        for c in range(1, chains):
            m_new = jnp.maximum(m_new, m_sc[c])
        for st in s_cols:
            m_new = jnp.maximum(m_new, st)
        l = jnp.zeros((rows, 1), _F32)
        acc = jnp.zeros((rows, KV_RANK), _F32)
        for c in range(chains):
            alpha = jnp.exp2(m_sc[c] - m_new)
            l = l + l_sc[c] * alpha
            acc = acc + acc_sc[c] * alpha
        for t in range(s_new):
            pt = jnp.exp2(s_cols[t] - m_new)
            l = l + pt
            acc = acc + pt.astype(_BF16).astype(_F32) * cn[t:t + 1, :]
        o = (acc * (1.0 / l)).astype(_BF16)
        full = _dot(o, wv_ref[...])
        head = lax.broadcasted_iota(jnp.int32, (rows, ATTN_WIDTH), 0) % N_HEADS
        lane_head = lax.broadcasted_iota(jnp.int32, (rows, ATTN_WIDTH), 1) // V_HEAD_DIM
        full = jnp.where(head == lane_head, full, 0.0)
        a = jnp.concatenate(
            [jnp.sum(full[t * N_HEADS:(t + 1) * N_HEADS], axis=0, keepdims=True) for t in range(s_new)],
            axis=0)
        o_ref[0] = _rms(a, g_ref[...]).astype(_BF16)


def _sample_attention(page_table, ql, qr, c_new, k_new, w_v, grp_g, cache_ckv, cache_krt, *, pages, chains):
    nb, n_pages = page_table.shape
    per_step = pages * chains
    n_steps = n_pages // per_step
    s_new = c_new.shape[1]
    rows = s_new * N_HEADS
    grid_spec = pltpu.PrefetchScalarGridSpec(
        num_scalar_prefetch=1,
        grid=(nb, n_steps),
        in_specs=[
            pl.BlockSpec((1, rows, KV_RANK), lambda b, j, pt: (b, 0, 0)),
            pl.BlockSpec((1, rows, QK_ROPE_DIM), lambda b, j, pt: (b, 0, 0)),
            pl.BlockSpec((1, s_new, KV_RANK), lambda b, j, pt: (b, 0, 0)),
            pl.BlockSpec((1, s_new, QK_ROPE_DIM), lambda b, j, pt: (b, 0, 0)),
            _const_spec((KV_RANK, ATTN_WIDTH)),
            _const_spec((1, ATTN_WIDTH)),
            pl.BlockSpec(memory_space=pl.ANY),
            pl.BlockSpec(memory_space=pl.ANY),
        ],
        out_specs=pl.BlockSpec((1, s_new, ATTN_WIDTH), lambda b, j, pt: (b, 0, 0)),
        scratch_shapes=[
            pltpu.VMEM((PAGE_RING, per_step, PAGE_SIZE, KV_RANK), _F32),
            pltpu.VMEM((PAGE_RING, per_step, QK_ROPE_DIM, PAGE_SIZE), _F32),
            pltpu.SemaphoreType.DMA((2, PAGE_RING)),
            pltpu.VMEM((chains, rows, 1), _F32),
            pltpu.VMEM((chains, rows, 1), _F32),
            pltpu.VMEM((chains, rows, KV_RANK), _F32),
        ],
    )
    return pl.pallas_call(
        functools.partial(_sample_attn_kernel, pages=pages, chains=chains, n_steps=n_steps, s_new=s_new),
        grid_spec=grid_spec,
        out_shape=jax.ShapeDtypeStruct((nb, s_new, ATTN_WIDTH), _BF16),
        compiler_params=pltpu.CompilerParams(
            dimension_semantics=("arbitrary", "arbitrary"), vmem_limit_bytes=VMEM_LIMIT),
        name="sample_attn",
    )(page_table, ql, qr, c_new, k_new, w_v, grp_g, cache_ckv, cache_krt)


def _mix_tail(y, h, a_n, cb_ref, lng_ref, lnb_ref, cg_ref, woa_ref, woc_ref):
    y = y + cb_ref[...]
    mu = jnp.mean(y, axis=-1, keepdims=True)
    yc = y - mu
    var = jnp.mean(yc * yc, axis=-1, keepdims=True)
    y = yc * lax.rsqrt(var + EPS) * lng_ref[...] + lnb_ref[...]
    y = y * _sigmoid(y)
    cv = _rms(y, cg_ref[...]).astype(_BF16)
    return h + _dot(a_n, woa_ref[...]) + _dot(cv, woc_ref[...])


def _shift_copies(ext_sc, sh_sc):
    n = sh_sc.shape[1]
    for r in range(1, SUBLANES):
        sh_sc[r - 1] = ext_sc[r:r + n, :]


def _conv_rows(ext_sc, sh_sc, cw_ref, r0):
    off = HALO - (CONV_WIDTH - 1)
    tiles = []
    for c0 in range(0, CONV_CH, LANES):
        cs = slice(c0, c0 + LANES)
        y = jnp.zeros((CONV_ROWS, LANES), _F32)
        for k in range(CONV_WIDTH):
            res = (off + k) % SUBLANES
            rows = slice(r0 + off + k - res, r0 + off + k - res + CONV_ROWS)
            src = ext_sc[rows, cs] if res == 0 else sh_sc[res - 1, rows, cs]
            y = y + src * cw_ref[k:k + 1, cs]
        tiles.append(y)
    return jnp.concatenate(tiles, axis=-1)


def _mix_prompt_kernel(h_ref, a_ref, glu_ref, prev_ref, first_ref, cw_ref, cb_ref, lng_ref, lnb_ref,
                       cg_ref, woa_ref, woc_ref, o_ref, ext_sc, sh_sc, y_sc, *, tm):
    i = pl.program_id(1)
    ext_sc[:HALO, :] = jnp.where(i == 0, first_ref[...], prev_ref[...])
    ext_sc[HALO:, :] = glu_ref[...]
    _shift_copies(ext_sc, sh_sc)
    for r0 in range(0, tm, CONV_ROWS):
        y_sc[r0:r0 + CONV_ROWS, :] = _conv_rows(ext_sc, sh_sc, cw_ref, r0)
    o_ref[...] = _mix_tail(y_sc[...], h_ref[...], a_ref[...], cb_ref, lng_ref, lnb_ref, cg_ref,
                           woa_ref, woc_ref)


def _mix_prompt(h, a_n, glu, first_halo, mw, *, batch, seq, tm):
    nt = seq // tm
    d = h.shape[1]
    hb = tm // HALO
    row = lambda b, i: (b * nt + i, 0)
    return pl.pallas_call(
        functools.partial(_mix_prompt_kernel, tm=tm),
        grid=(batch, nt),
        in_specs=[
            pl.BlockSpec((tm, d), row),
            pl.BlockSpec((tm, ATTN_WIDTH), row),
            pl.BlockSpec((tm, CONV_CH), row),
            pl.BlockSpec((HALO, CONV_CH), lambda b, i: (jnp.maximum((b * nt + i) * hb - 1, 0), 0)),
            _const_spec((HALO, CONV_CH)),
            _const_spec(mw["conv_w"].shape), _const_spec(mw["conv_b"].shape),
            _const_spec(mw["ln_g"].shape), _const_spec(mw["ln_b"].shape),
            _const_spec(mw["conv_grp"].shape),
            _const_spec(mw["w_out_a"].shape), _const_spec(mw["w_out_c"].shape),
        ],
        out_specs=pl.BlockSpec((tm, d), row),
        out_shape=jax.ShapeDtypeStruct(h.shape, _F32),
        scratch_shapes=[pltpu.VMEM((HALO + tm, CONV_CH), _F32),
                        pltpu.VMEM((SUBLANES - 1, HALO + tm - SUBLANES, CONV_CH), _F32),
                        pltpu.VMEM((tm, CONV_CH), _F32)],
        compiler_params=pltpu.CompilerParams(
            dimension_semantics=("arbitrary", "arbitrary"), vmem_limit_bytes=VMEM_LIMIT),
        name="mix_prompt",
    )(h, a_n, glu, glu, first_halo, mw["conv_w"], mw["conv_b"], mw["ln_g"], mw["ln_b"],
      mw["conv_grp"], mw["w_out_a"], mw["w_out_c"])


def _mix_sample_kernel(h_ref, a_ref, full_ref, cw_ref, cb_ref, lng_ref, lnb_ref,
                       cg_ref, woa_ref, woc_ref, o_ref, *, s_new):
    ys = []
    for t in range(s_new):
        y = jnp.zeros(full_ref.shape[1:], _F32)
        for k in range(CONV_WIDTH):
            y = y + full_ref[t + k] * cw_ref[k:k + 1, :]
        ys.append(y)
    y = jnp.concatenate(ys, axis=0)
    o_ref[...] = _mix_tail(y, h_ref[...], a_ref[...], cb_ref, lng_ref, lnb_ref, cg_ref, woa_ref, woc_ref)


def _mix_sample(h, a_n, full_t, mw, *, s_new):
    t, d = h.shape
    return pl.pallas_call(
        functools.partial(_mix_sample_kernel, s_new=s_new),
        grid=(1,),
        in_specs=[
            _const_spec((t, d)), _const_spec((t, ATTN_WIDTH)), _const_spec(full_t.shape),
            _const_spec(mw["conv_w"].shape), _const_spec(mw["conv_b"].shape),
            _const_spec(mw["ln_g"].shape), _const_spec(mw["ln_b"].shape),
            _const_spec(mw["conv_grp"].shape),
            _const_spec(mw["w_out_a"].shape), _const_spec(mw["w_out_c"].shape),
        ],
        out_specs=pl.BlockSpec((t, d), lambda i: (0, 0)),
        out_shape=jax.ShapeDtypeStruct((t, d), _F32),
        compiler_params=pltpu.CompilerParams(
            dimension_semantics=("arbitrary",), vmem_limit_bytes=VMEM_LIMIT),
        name="mix_sample",
    )(h, a_n, full_t, mw["conv_w"], mw["conv_b"], mw["ln_g"], mw["ln_b"],
      mw["conv_grp"], mw["w_out_a"], mw["w_out_c"])


def _rope_tables(pos):
    half = QK_ROPE_DIM // 2
    inv = ROPE_THETA ** (-jnp.arange(half, dtype=_F32) / half)
    ang = pos.astype(_F32)[:, None] * inv[None, :]
    cos = jnp.cos(ang)
    sin = jnp.sin(ang)
    n = pos.shape[0]
    cos2 = jnp.concatenate([cos, cos], axis=-1)
    sin2 = jnp.concatenate([sin, sin], axis=-1)
    pad_q = jnp.zeros((n, HEAD_SLOT - Q_HEAD_DIM), _F32)
    cosq = SCORE_SCALE * jnp.concatenate([jnp.ones((n, QK_NOPE_DIM), _F32), cos2, pad_q], axis=-1)
    sinq = SCORE_SCALE * jnp.concatenate([jnp.zeros((n, QK_NOPE_DIM), _F32), sin2, pad_q], axis=-1)
    pad_k = jnp.zeros((n, LANES - QK_ROPE_DIM), _F32)
    cosk = jnp.concatenate([cos2, pad_k], axis=-1)
    sink = jnp.concatenate([sin2, pad_k], axis=-1)
    return cosq, sinq, cosk, sink


def _rot_cols(w):
    half = QK_ROPE_DIM // 2
    return jnp.concatenate([-w[..., half:], w[..., :half]], axis=-1)


def _layer_weights(l, mix_norm, w_in, q_norm, w_uq, kv_norm, w_uk, w_uv):
    d = w_in.shape[1]
    win = w_in[l]
    o_kv = Q_RANK
    o_kr = o_kv + KV_RANK
    o_cv = o_kr + QK_ROPE_DIM
    w_kr = win[:, o_kr:o_cv]
    zpad = jnp.zeros((d, LANES - QK_ROPE_DIM), _F32)
    win_p = jnp.concatenate([win[:, :o_kr], win[:, o_cv:], w_kr, zpad, _rot_cols(w_kr), zpad], axis=-1)

    wq = w_uq[l].reshape(Q_RANK, N_HEADS, Q_HEAD_DIM)
    zq = jnp.zeros((Q_RANK, N_HEADS, HEAD_SLOT - Q_HEAD_DIM), _F32)
    w_qa = jnp.concatenate([wq, zq], axis=-1).reshape(Q_RANK, N_HEADS * HEAD_SLOT)
    w_qb = jnp.concatenate([jnp.zeros((Q_RANK, N_HEADS, QK_NOPE_DIM), _F32),
                            _rot_cols(wq[..., QK_NOPE_DIM:]), zq], axis=-1).reshape(Q_RANK, N_HEADS * HEAD_SLOT)

    wuk = w_uk[l]
    zk = jnp.zeros((KV_RANK, N_HEADS, HEAD_SLOT - QK_NOPE_DIM), _F32)
    w_k1 = jnp.concatenate([wuk, zk], axis=-1).reshape(KV_RANK, N_HEADS * HEAD_SLOT)
    eye = jnp.eye(QK_ROPE_DIM, dtype=_F32)
    place = jnp.concatenate([jnp.zeros((QK_ROPE_DIM, QK_NOPE_DIM), _F32), eye,
                             jnp.zeros((QK_ROPE_DIM, HEAD_SLOT - Q_HEAD_DIM), _F32)], axis=-1)
    w_v =w_uv[l].reshape(KV_RANK, ATTN_WIDTH)

    w_vt = jnp.concatenate([w_uv[l].transpose(1, 2, 0), jnp.zeros((N_HEADS, VT_ONES, KV_RANK), _F32)],
                           axis=1).reshape(N_HEADS * VT_ROWS, KV_RANK)
    vt_bias = jnp.tile(jnp.concatenate([jnp.zeros((V_HEAD_DIM, 1), _F32), jnp.ones((VT_ONES, 1), _F32)],
                                       axis=0), (N_HEADS, 1))

    head_eye = jnp.eye(N_HEADS, dtype=_F32)[:, None, :, None]
    slot_abs = jnp.concatenate([wuk.transpose(1, 2, 0),
                                jnp.zeros((N_HEADS, HEAD_SLOT - QK_NOPE_DIM, KV_RANK), _F32)], axis=1)
    w_abs = (head_eye * slot_abs[:, :, None, :]).reshape(N_HEADS * HEAD_SLOT, N_HEADS * KV_RANK)
    w_qr = (head_eye * place.T[None, :, None, :]).reshape(N_HEADS * HEAD_SLOT, N_HEADS * QK_ROPE_DIM)

    bf = lambda x: x.astype(_BF16)
    return {
        "mix_norm": mix_norm[l][None], "w_in": bf(win_p), "q_norm": q_norm[l][None],
        "w_qa": bf(w_qa), "w_qb": bf(w_qb), "kv_norm": kv_norm[l][None],
        "w_k1": bf(w_k1), "w_v": bf(w_v), "w_vt": bf(w_vt), "vt_bias": vt_bias,
        "w_abs": bf(w_abs), "w_qr": bf(w_qr),
    }


def _pick_tile(n, prefer):
    for t in prefer:
        if n % t == 0:
            return t
    return n


def kernel(x_prompt, x_sample, cache_kv_latent, cache_k_rope, state_conv, page_table, meta_tokens,
           ffn1_norm, ffn1_w_gate, ffn1_w_up, ffn1_w_down, mix_norm, w_in, q_norm, w_uq, kv_norm,
           w_uk, w_uv, conv_w, conv_b, conv_ln_g, conv_ln_b, attn_grp_norm, conv_grp_norm, w_out,
           ffn2_norm, ffn2_w_gate, ffn2_w_up, ffn2_w_down, final_norm):
    b_p, s_p, d = x_prompt.shape
    b_s, s_s, _ = x_sample.shape
    depth = ffn1_norm.shape[0]
    assert depth == 1, "the meta rows' later-layer inputs are not carried; single layer only"
    n_pages = page_table.shape[1]
    past_len = n_pages * PAGE_SIZE
    assert CONV_WIDTH - 1 <= HALO and N_META <= HALO
    l = 0
    bf = lambda x: x.astype(_BF16)

    pw = _layer_weights(l, mix_norm, w_in, q_norm, w_uq, kv_norm, w_uk, w_uv)
    f1 = (ffn1_norm[l][None], bf(ffn1_w_gate[l]), bf(ffn1_w_up[l]), bf(ffn1_w_down[l]))
    f2 = (ffn2_norm[l][None], bf(ffn2_w_gate[l]), bf(ffn2_w_up[l]), bf(ffn2_w_down[l]))
    fin = final_norm[None]
    mw = {
        "conv_w": jnp.concatenate([conv_w[l], jnp.zeros((1, CONV_CH), _F32)], axis=0),
        "conv_b": conv_b[l][None], "ln_g": conv_ln_g[l][None], "ln_b": conv_ln_b[l][None],
        "conv_grp": conv_grp_norm[l][None],
        "w_out_a": bf(w_out[l][:ATTN_WIDTH]), "w_out_c": bf(w_out[l][ATTN_WIDTH:]),
    }

    tm_p = _pick_tile(s_p, (512, 256, 128))
    tq = _pick_tile(s_p, (1024, 512, 256, 128))
    tk = _pick_tile(tq, (256, 128))
    assert tm_p % CONV_ROWS == 0 and tm_p % HALO == 0

    h_m = _ffn(meta_tokens, *f1, fin, tm=N_META, final=False)
    tab_m = _rope_tables(jnp.arange(N_META, dtype=jnp.int32))
    ckv_m, kr_m, glu_m, _, k_m, vt_m = _proj(h_m, tab_m, 1, pw, tm=N_META, absorbed=False, tkv=N_META)
    first_halo = jnp.concatenate([jnp.zeros((HALO - N_META, CONV_CH), _F32), glu_m], axis=0)

    xp = x_prompt.reshape(b_p * s_p, d)
    h_p = _ffn(xp, *f1, fin, tm=tm_p, final=False)
    tab_p = _rope_tables(N_META + jnp.arange(s_p, dtype=jnp.int32))
    ckv_p, kr_p, glu_p, q_p, k_p, vt_p = _proj(h_p, tab_p, s_p // tm_p, pw, tm=tm_p, absorbed=False, tkv=tk)
    a_p = _prompt_attention(q_p, k_p, vt_p, k_m, vt_m[0], attn_grp_norm[l][:, None],
                            batch=b_p, seq=s_p, tq=tq, tk=tk)
    h2_p = _mix_prompt(h_p, a_p, glu_p, first_halo, mw, batch=b_p, seq=s_p, tm=tm_p)
    y_p = _ffn(h2_p, *f2, fin, tm=tm_p, final=True).reshape(b_p, s_p, d)

    n_s = b_s * s_s
    xs = x_sample.transpose(1, 0, 2).reshape(n_s, d)
    h_s = _ffn(xs, *f1, fin, tm=n_s, final=False)
    pos_s = past_len + jnp.repeat(jnp.arange(s_s, dtype=jnp.int32), b_s)
    ckv_s, kr_s, glu_s, ql_s, qr_s = _proj(h_s, _rope_tables(pos_s), 1, pw, tm=n_s, absorbed=True)
    seq_major = lambda x: x.reshape(s_s, b_s, -1).transpose(1, 0, 2)
    ckv_sb, kr_sb, glu_sb = seq_major(ckv_s), seq_major(kr_s), seq_major(glu_s)
    ql_b = seq_major(ql_s).reshape(b_s, s_s * N_HEADS, KV_RANK)
    qr_b = seq_major(qr_s).reshape(b_s, s_s * N_HEADS, QK_ROPE_DIM)
    per_step = _pick_tile(n_pages, (128, 64, 32, 16, 8, 4, 2))
    chains = _pick_tile(per_step, (4, 2))
    a_s = _sample_attention(page_table, ql_b, qr_b, ckv_sb, kr_sb, pw["w_v"], attn_grp_norm[l][None],
                            cache_kv_latent[l], jnp.swapaxes(cache_k_rope[l], 1, 2),
                            pages=per_step // chains, chains=chains)
    a_st = a_s.transpose(1, 0, 2).reshape(n_s, ATTN_WIDTH)
    full_t = jnp.concatenate([state_conv[l].transpose(1, 0, 2), glu_s.reshape(s_s, b_s, CONV_CH)], axis=0)
    h2_s = _mix_sample(h_s, a_st, full_t, mw, s_new=s_s)
    y_s = _ffn(h2_s, *f2, fin, tm=n_s, final=True).reshape(s_s, b_s, d).transpose(1, 0, 2)

    bcast = lambda x: jnp.broadcast_to(x[None], (b_p,) + x.shape)
    new_ckv_p = jnp.concatenate([bcast(ckv_m), ckv_p.reshape(b_p, s_p, KV_RANK)], axis=1)[None]
    new_kr_p = jnp.concatenate([bcast(kr_m), kr_p.reshape(b_p, s_p, QK_ROPE_DIM)], axis=1)[None]
    n_keep = CONV_WIDTH - 1
    new_conv_p = glu_p.reshape(b_p, s_p, CONV_CH)[:, s_p - n_keep:][None]
    new_conv_s = jnp.concatenate([state_conv[l], glu_sb], axis=1)[:, -n_keep:][None]
    return (y_p, y_s, new_ckv_p, new_kr_p, new_conv_p, ckv_sb[None], kr_sb[None], new_conv_s)
```

```python
import functools
import math

import jax
import jax.numpy as jnp
from jax import lax
from jax.experimental import pallas as pl
from jax.experimental.pallas import tpu as pltpu

N_META = 16
N_HEADS = 8
QK_NOPE_DIM = 64
QK_ROPE_DIM = 32
V_HEAD_DIM = 64
Q_RANK = 256
KV_RANK = 256
CONV_CH = 512
CONV_WIDTH = 31
ROPE_THETA = 10000.0
EPS = 1e-6
PAGE_SIZE = 128
Q_HEAD_DIM = QK_NOPE_DIM + QK_ROPE_DIM
ATTN_WIDTH = N_HEADS * V_HEAD_DIM
SOFTMAX_SCALE = Q_HEAD_DIM ** -0.5
SCORE_SCALE = SOFTMAX_SCALE * math.log2(math.e)

LANES = 128
SUBLANES = 8
HEAD_SLOT = LANES
HALO = 32
CONV_ROWS = 64
HEAD_LOOKAHEAD = 3
FFN_CHUNK_ELEMS = 512 * 1408
PAGE_RING = 2
VT_ONES = 16
VT_ROWS = V_HEAD_DIM + VT_ONES
VMEM_LIMIT = 56 * 1024 * 1024

_F32 = jnp.float32
_BF16 = jnp.bfloat16


def _const_spec(shape):
    nd = len(shape)
    return pl.BlockSpec(shape, lambda *_: (0,) * nd, pipeline_mode=pl.Buffered(1))


def _rms(x, g):
    return x * lax.rsqrt(jnp.mean(x * x, axis=-1, keepdims=True) + EPS) * g


def _sigmoid(x):
    return 1.0 / (1.0 + jnp.exp(-x))


def _dot(a, b):
    return jnp.dot(a, b, preferred_element_type=_F32)


def _dot_nt(a, b):
    return lax.dot_general(a, b, (((1,), (1,)), ((), ())), preferred_element_type=_F32)


def _ffn_kernel(x_ref, g_ref, wg_ref, wu_ref, wd_ref, fg_ref, o_ref, *, ff_chunks, final):
    x = x_ref[...]
    xn = _rms(x, g_ref[...]).astype(_BF16)
    acc = jnp.zeros_like(x)
    for c0, cw in ff_chunks:
        gate = _dot(xn, wg_ref[:, c0:c0 + cw])
        up = _dot(xn, wu_ref[:, c0:c0 + cw])
        act = (gate * _sigmoid(gate) * up).astype(_BF16)
        acc = acc + _dot(act, wd_ref[c0:c0 + cw, :])
    y = x + 0.5 * acc
    if final:
        y = _rms(y, fg_ref[...])
    o_ref[...] = y


def _ffn(x, norm_g, wg, wu, wd, final_g, *, tm, final):
    t, d = x.shape
    d_ff = wg.shape[1]
    n_lane_tiles = d_ff // LANES
    n_chunks = max(1, min(n_lane_tiles, -(-tm * d_ff // FFN_CHUNK_ELEMS)))
    bounds = [(n_lane_tiles * c // n_chunks) * LANES for c in range(n_chunks + 1)]
    ff_chunks = tuple((bounds[c], bounds[c + 1] - bounds[c]) for c in range(n_chunks))
    return pl.pallas_call(
        functools.partial(_ffn_kernel, ff_chunks=ff_chunks, final=final),
        grid=(t // tm,),
        in_specs=[
            pl.BlockSpec((tm, d), lambda i: (i, 0)),
            _const_spec((1, d)),
            _const_spec((d, d_ff)),
            _const_spec((d, d_ff)),
            _const_spec((d_ff, d)),
            _const_spec((1, d)),
        ],
        out_specs=pl.BlockSpec((tm, d), lambda i: (i, 0)),
        out_shape=jax.ShapeDtypeStruct((t, d), _F32),
        compiler_params=pltpu.CompilerParams(
            dimension_semantics=("arbitrary",), vmem_limit_bytes=VMEM_LIMIT),
        name="ffn_final" if final else "ffn",
    )(x, norm_g, wg, wu, wd, final_g)


def _proj_kernel(h_ref, cosq_ref, sinq_ref, cosk_ref, sink_ref,
                 mixg_ref, win_ref, qg_ref, wqa_ref, wqb_ref, kvg_ref, *rest, absorbed):
    if absorbed:
        wabs_ref, wqr_ref, ckv_ref, kr_ref, glu_ref, ql_ref, qr_ref = rest
    else:
        wk1_ref, wvt_ref, vtb_ref, ckv_ref, kr_ref, glu_ref, q_ref, k_ref, vt_ref = rest
    n = _rms(h_ref[...], mixg_ref[...]).astype(_BF16)
    z = _dot(n, win_ref[...])
    o_kv = Q_RANK
    o_ca = o_kv + KV_RANK
    o_cb = o_ca + CONV_CH
    o_ka = o_cb + CONV_CH
    o_kb = o_ka + LANES

    glu_ref[...] = z[:, o_ca:o_cb] * _sigmoid(z[:, o_cb:o_ka])

    c = _rms(z[:, o_kv:o_ca], kvg_ref[...])
    ckv_ref[...] = c
    c_bf = c.astype(_BF16)
    kr = z[:, o_ka:o_kb] * cosk_ref[...] + z[:, o_kb:o_kb + LANES] * sink_ref[...]
    kr_ref[...] = kr[:, :QK_ROPE_DIM]

    qn = _rms(z[:, :Q_RANK], qg_ref[...]).astype(_BF16)
    qa = _dot(qn, wqa_ref[...])
    qb = _dot(qn, wqb_ref[...])
    cosq = cosq_ref[...]
    sinq = sinq_ref[...]
    q_heads = []
    for h in range(N_HEADS):
        sl = slice(h * HEAD_SLOT, (h + 1) * HEAD_SLOT)
        q_heads.append((qa[:, sl] * cosq + qb[:, sl] * sinq).astype(_BF16))
    q = jnp.concatenate(q_heads, axis=-1)
    if absorbed:
        ql_ref[...] = _dot(q, wabs_ref[...]).astype(_BF16)
        qr_ref[...] = _dot(q, wqr_ref[...]).astype(_BF16)
    else:
        q_ref[...] = q
        k_nope = _dot(c_bf, wk1_ref[...])
        kr_slot = pltpu.roll(kr, QK_NOPE_DIM, 1)
        k_ref[...] = jnp.concatenate(
            [(k_nope[:, h * HEAD_SLOT:(h + 1) * HEAD_SLOT] + kr_slot).astype(_BF16) for h in range(N_HEADS)],
            axis=-1)
        vt = (_dot_nt(wvt_ref[...], c_bf) + vtb_ref[...]).astype(_BF16)
        tkv = vt_ref.shape[2]
        for blk in range(vt_ref.shape[0]):
            vt_ref[blk] = vt[:, blk * tkv:(blk + 1) * tkv]


def _proj(h, tables, n_table_blocks, pw, *, tm, absorbed, tkv=None):
    t, d = h.shape
    cosq, sinq, cosk, sink = tables
    row = lambda i: (i, 0)
    tab = lambda i: (i % n_table_blocks, 0)
    in_specs = [
        pl.BlockSpec((tm, d), row),
        pl.BlockSpec((tm, LANES), tab), pl.BlockSpec((tm, LANES), tab),
        pl.BlockSpec((tm, LANES), tab), pl.BlockSpec((tm, LANES), tab),
        _const_spec(pw["mix_norm"].shape), _const_spec(pw["w_in"].shape),
        _const_spec(pw["q_norm"].shape), _const_spec(pw["w_qa"].shape),
        _const_spec(pw["w_qb"].shape), _const_spec(pw["kv_norm"].shape),
    ]
    args = [h, cosq, sinq, cosk, sink, pw["mix_norm"], pw["w_in"], pw["q_norm"],
            pw["w_qa"], pw["w_qb"], pw["kv_norm"]]
    out_shape = [jax.ShapeDtypeStruct((t, KV_RANK), _F32),
                 jax.ShapeDtypeStruct((t, QK_ROPE_DIM), _F32),
                 jax.ShapeDtypeStruct((t, CONV_CH), _F32)]
    out_specs = [pl.BlockSpec((tm, KV_RANK), row), pl.BlockSpec((tm, QK_ROPE_DIM), row),
                 pl.BlockSpec((tm, CONV_CH), row)]
    if absorbed:
        extra = ["w_abs", "w_qr"]
        widths = [N_HEADS * KV_RANK, N_HEADS * QK_ROPE_DIM]
    else:
        extra = ["w_k1", "w_vt", "vt_bias"]
        widths = [N_HEADS * HEAD_SLOT, N_HEADS * HEAD_SLOT]
    for name in extra:
        in_specs.append(_const_spec(pw[name].shape))
        args.append(pw[name])
    for w in widths:
        out_shape.append(jax.ShapeDtypeStruct((t, w), _BF16))
        out_specs.append(pl.BlockSpec((tm, w), row))
    if not absorbed:
        vr = N_HEADS * VT_ROWS
        out_shape.append(jax.ShapeDtypeStruct((t // tkv, vr, tkv), _BF16))
        out_specs.append(pl.BlockSpec((tm // tkv, vr, tkv), lambda i: (i, 0, 0)))
    return pl.pallas_call(
        functools.partial(_proj_kernel, absorbed=absorbed),
        grid=(t // tm,),
        in_specs=in_specs,
        out_specs=out_specs,
        out_shape=out_shape,
        compiler_params=pltpu.CompilerParams(
            dimension_semantics=("arbitrary",), vmem_limit_bytes=VMEM_LIMIT),
        name="proj_absorbed" if absorbed else "proj",
    )(*args)


def _prompt_attn_kernel(q_ref, k_ref, vt_ref, km_ref, vtm_ref, g_ref, o_ref, m_sc, acc_sc, *, tq, tk):
    i = pl.program_id(1)
    n_full = i * (tq // tk)

    def q_slot(h):
        return slice(h * HEAD_SLOT, (h + 1) * HEAD_SLOT)

    def v_rows(h):
        return slice(h * VT_ROWS, (h + 1) * VT_ROWS)

    def update(h, s, vt_t, c0, mask, with_meta):
        rs = v_rows(h)
        if mask is not None:
            s_diag = jnp.where(mask, s[:, :tk], -jnp.inf)
            s = s_diag if s.shape[1] == tk else jnp.concatenate([s_diag, s[:, tk:]], axis=1)
        s_max = jnp.max(s, axis=0, keepdims=True)
        if with_meta:
            p = jnp.exp2(s - s_max).astype(_BF16)
            acc_sc[rs, :] = _dot(vtm_ref[rs, :], p[:N_META]) + _dot(vt_t, p[N_META:])
            m_sc[h] = s_max
        else:
            m_old = m_sc[h, :, c0:]
            m_new = jnp.maximum(m_old, s_max)
            alpha = jnp.exp2(m_old - m_new)
            p = jnp.exp2(s - m_new)
            acc_sc[rs, c0:] = acc_sc[rs, c0:] * alpha + _dot(vt_t, p.astype(_BF16))
            m_sc[h, :, c0:] = m_new

    def tile(j, c0=0, mask=None, with_meta=False):
        ks = pl.ds(pl.multiple_of(j * tk, tk), tk)

        def scores(h):
            k_t = k_ref[ks, q_slot(h)]
            if with_meta:
                k_t = jnp.concatenate([km_ref[:, q_slot(h)], k_t], axis=0)
            return _dot_nt(k_t, q_ref[c0:, q_slot(h)])

        pending = [scores(h) for h in range(HEAD_LOOKAHEAD)]
        for h in range(N_HEADS):
            s = pending.pop(0)
            if h + HEAD_LOOKAHEAD < N_HEADS:
                pending.append(scores(h + HEAD_LOOKAHEAD))
            update(h, s, vt_ref[j, v_rows(h), :], c0, mask, with_meta)

    for d in range(tq // tk):
        c0 = d * tk
        n_meta = N_META if d == 0 else 0
        key_r = lax.broadcasted_iota(jnp.int32, (n_meta + tk, tk), 0)
        qry_c = lax.broadcasted_iota(jnp.int32, (n_meta + tk, tk), 1)
        tile(n_full + d, c0, mask=key_r - n_meta <= qry_c, with_meta=d == 0)

    per_iter = tq // tk

    def full_tiles(jj, carry):
        for u in range(per_iter):
            tile(jj * per_iter + u)
        return carry

    lax.fori_loop(0, i, full_tiles, 0)

    heads = []
    for h in range(N_HEADS):
        r0 = h * VT_ROWS
        inv_l = 1.0 / acc_sc[r0 + V_HEAD_DIM:r0 + V_HEAD_DIM + 1, :]
        heads.append(acc_sc[r0:r0 + V_HEAD_DIM, :] * inv_l)
    a_t = jnp.concatenate(heads, axis=0)
    ms = jnp.mean(a_t * a_t, axis=0, keepdims=True)
    a_n = a_t * lax.rsqrt(ms + EPS) * g_ref[...]
    o_ref[...] = a_n.T.astype(_BF16)


def _prompt_attention(q, k, vt, k_meta, vt_meta, grp_g_col, *, batch, seq, tq, tk):
    nq = seq // tq
    nk = seq // tk
    kw = N_HEADS * HEAD_SLOT
    vr = N_HEADS * VT_ROWS
    return pl.pallas_call(
        functools.partial(_prompt_attn_kernel, tq=tq, tk=tk),
        grid=(batch, nq),
        in_specs=[
            pl.BlockSpec((tq, kw), lambda b, i: (b * nq + i, 0)),
            pl.BlockSpec((seq, kw), lambda b, i: (b, 0)),
            pl.BlockSpec((nk, vr, tk), lambda b, i: (b, 0, 0)),
            _const_spec(k_meta.shape),
            _const_spec(vt_meta.shape),
            _const_spec((ATTN_WIDTH, 1)),
        ],
        out_specs=pl.BlockSpec((tq, ATTN_WIDTH), lambda b, i: (b * nq + i, 0)),
        out_shape=jax.ShapeDtypeStruct((batch * seq, ATTN_WIDTH), _BF16),
        scratch_shapes=[
            pltpu.VMEM((N_HEADS, 1, tq), _F32),
            pltpu.VMEM((vr, tq), _F32),
        ],
        compiler_params=pltpu.CompilerParams(
            dimension_semantics=("arbitrary", "arbitrary"), vmem_limit_bytes=VMEM_LIMIT),
        name="prompt_attn",
    )(q, k, vt, k_meta, vt_meta, grp_g_col)


def _sample_attn_kernel(pt_ref, ql_ref, qr_ref, cn_ref, kn_ref, wv_ref, g_ref, ckv_hbm, krt_hbm,
                        o_ref, cbuf, kbuf, sem, m_sc, l_sc, acc_sc, *, pages, chains, n_steps, s_new):
    b = pl.program_id(0)
    j = pl.program_id(1)
    nb = pl.num_programs(0)
    step = b * n_steps + j
    n_slots = cbuf.shape[0]
    slot = step % n_slots
    rows = s_new * N_HEADS
    per_step = pages * chains

    def page_copy(page, p, sl):
        return (pltpu.make_async_copy(ckv_hbm.at[page], cbuf.at[sl, p], sem.at[0, sl]),
                pltpu.make_async_copy(krt_hbm.at[page], kbuf.at[sl, p], sem.at[1, sl]))

    def start_pages(st):
        bb, jj, sl = st // n_steps, st % n_steps, st % n_slots
        for p in range(per_step):
            for cp in page_copy(pt_ref[bb, jj * per_step + p], p, sl):
                cp.start()

    @pl.when(step == 0)
    def _():
        for st in range(n_slots - 1):
            start_pages(st)

    @pl.when(step + n_slots - 1 < nb * n_steps)
    def _():
        start_pages(step + n_slots - 1)

    @pl.when(j == 0)
    def _():
        m_sc[...] = jnp.full(m_sc.shape, -jnp.inf, _F32)
        l_sc[...] = jnp.zeros(l_sc.shape, _F32)
        acc_sc[...] = jnp.zeros(acc_sc.shape, _F32)

    for p in range(per_step):
        for cp in page_copy(0, p, slot):
            cp.wait()

    ql = ql_ref[0]
    qr = qr_ref[0]
    n_keys = pages * PAGE_SIZE
    latents, scores = [], []
    for c in range(chains):
        c_bf = cbuf[slot, c * pages:(c + 1) * pages].reshape(n_keys, KV_RANK).astype(_BF16)
        krt = jnp.concatenate([kbuf[slot, c * pages + p] for p in range(pages)], axis=1).astype(_BF16)
        latents.append(c_bf)
        scores.append(_dot_nt(ql, c_bf) + _dot(qr, krt))
    for c in range(chains):
        c_bf, s = latents[c], scores[c]
        m = m_sc[c]
        m_new = jnp.maximum(m, jnp.max(s, axis=-1, keepdims=True))
        alpha = jnp.exp2(m - m_new)
        p = jnp.exp2(s - m_new)
        l_sc[c] = l_sc[c] * alpha + jnp.sum(p, axis=-1, keepdims=True)
        acc_sc[c] = acc_sc[c] * alpha + _dot(p.astype(_BF16), c_bf)
        m_sc[c] = m_new

    @pl.when(j == n_steps - 1)
    def _():
        qlf = ql.astype(_F32)
        qrf = qr.astype(_F32)
        tok = lax.broadcasted_iota(jnp.int32, (rows, 1), 0) // N_HEADS
        cn = cn_ref[0].astype(_BF16).astype(_F32)
        kn = kn_ref[0].astype(_BF16).astype(_F32)
        s_cols = []
        for t in range(s_new):
            st = (jnp.sum(qlf * cn[t:t + 1, :], axis=-1, keepdims=True)
                  + jnp.sum(qrf * kn[t:t + 1, :], axis=-1, keepdims=True))
            s_cols.append(jnp.where(tok >= t, st, -jnp.inf))
        m_new = m_sc[0]
        for c in range(1, chains):
            m_new = jnp.maximum(m_new, m_sc[c])
        for st in s_cols:
            m_new = jnp.maximum(m_new, st)
        l = jnp.zeros((rows, 1), _F32)
        acc = jnp.zeros((rows, KV_RANK), _F32)
        for c in range(chains):
            alpha = jnp.exp2(m_sc[c] - m_new)
            l = l + l_sc[c] * alpha
            acc = acc + acc_sc[c] * alpha
        for t in range(s_new):
            pt = jnp.exp2(s_cols[t] - m_new)
            l = l + pt
            acc = acc + pt.astype(_BF16).astype(_F32) * cn[t:t + 1, :]
        o = (acc * (1.0 / l)).astype(_BF16)
        full = _dot(o, wv_ref[...])
        head = lax.broadcasted_iota(jnp.int32, (rows, ATTN_WIDTH), 0) % N_HEADS
        lane_head = lax.broadcasted_iota(jnp.int32, (rows, ATTN_WIDTH), 1) // V_HEAD_DIM
        full = jnp.where(head == lane_head, full, 0.0)
        a = jnp.concatenate(
            [jnp.sum(full[t * N_HEADS:(t + 1) * N_HEADS], axis=0, keepdims=True) for t in range(s_new)],
            axis=0)
        o_ref[0] = _rms(a, g_ref[...]).astype(_BF16)


def _sample_attention(page_table, ql, qr, c_new, k_new, w_v, grp_g, cache_ckv, cache_krt, *, pages, chains):
    nb, n_pages = page_table.shape
    per_step = pages * chains
    n_steps = n_pages // per_step
    s_new = c_new.shape[1]
    rows = s_new * N_HEADS
    grid_spec = pltpu.PrefetchScalarGridSpec(
        num_scalar_prefetch=1,
        grid=(nb, n_steps),
        in_specs=[
            pl.BlockSpec((1, rows, KV_RANK), lambda b, j, pt: (b, 0, 0)),
            pl.BlockSpec((1, rows, QK_ROPE_DIM), lambda b, j, pt: (b, 0, 0)),
            pl.BlockSpec((1, s_new, KV_RANK), lambda b, j, pt: (b, 0, 0)),
            pl.BlockSpec((1, s_new, QK_ROPE_DIM), lambda b, j, pt: (b, 0, 0)),
            _const_spec((KV_RANK, ATTN_WIDTH)),
            _const_spec((1, ATTN_WIDTH)),
            pl.BlockSpec(memory_space=pl.ANY),
            pl.BlockSpec(memory_space=pl.ANY),
        ],
        out_specs=pl.BlockSpec((1, s_new, ATTN_WIDTH), lambda b, j, pt: (b, 0, 0)),
        scratch_shapes=[
            pltpu.VMEM((PAGE_RING, per_step, PAGE_SIZE, KV_RANK), _F32),
            pltpu.VMEM((PAGE_RING, per_step, QK_ROPE_DIM, PAGE_SIZE), _F32),
            pltpu.SemaphoreType.DMA((2, PAGE_RING)),
            pltpu.VMEM((chains, rows, 1), _F32),
            pltpu.VMEM((chains, rows, 1), _F32),
            pltpu.VMEM((chains, rows, KV_RANK), _F32),
        ],
    )
    return pl.pallas_call(
        functools.partial(_sample_attn_kernel, pages=pages, chains=chains, n_steps=n_steps, s_new=s_new),
        grid_spec=grid_spec,
        out_shape=jax.ShapeDtypeStruct((nb, s_new, ATTN_WIDTH), _BF16),
        compiler_params=pltpu.CompilerParams(
            dimension_semantics=("arbitrary", "arbitrary"), vmem_limit_bytes=VMEM_LIMIT),
        name="sample_attn",
    )(page_table, ql, qr, c_new, k_new, w_v, grp_g, cache_ckv, cache_krt)


def _mix_tail(y, h, a_n, cb_ref, lng_ref, lnb_ref, cg_ref, woa_ref, woc_ref):
    y = y + cb_ref[...]
    mu = jnp.mean(y, axis=-1, keepdims=True)
    yc = y - mu
    var = jnp.mean(yc * yc, axis=-1, keepdims=True)
    y = yc * lax.rsqrt(var + EPS) * lng_ref[...] + lnb_ref[...]
    y = y * _sigmoid(y)
    cv = _rms(y, cg_ref[...]).astype(_BF16)
    return h + _dot(a_n, woa_ref[...]) + _dot(cv, woc_ref[...])


def _shift_copies(ext_sc, sh_sc):
    n = sh_sc.shape[1]
    for r in range(1, SUBLANES):
        sh_sc[r - 1] = ext_sc[r:r + n, :]


def _conv_rows(ext_sc, sh_sc, cw_ref, r0):
    off = HALO - (CONV_WIDTH - 1)
    tiles = []
    for c0 in range(0, CONV_CH, LANES):
        cs = slice(c0, c0 + LANES)
        y = jnp.zeros((CONV_ROWS, LANES), _F32)
        for k in range(CONV_WIDTH):
            res = (off + k) % SUBLANES
            rows = slice(r0 + off + k - res, r0 + off + k - res + CONV_ROWS)
            src = ext_sc[rows, cs] if res == 0 else sh_sc[res - 1, rows, cs]
            y = y + src * cw_ref[k:k + 1, cs]
        tiles.append(y)
    return jnp.concatenate(tiles, axis=-1)


def _mix_prompt_kernel(h_ref, a_ref, glu_ref, prev_ref, first_ref, cw_ref, cb_ref, lng_ref, lnb_ref,
                       cg_ref, woa_ref, woc_ref, o_ref, ext_sc, sh_sc, y_sc, *, tm):
    i = pl.program_id(1)
    ext_sc[:HALO, :] = jnp.where(i == 0, first_ref[...], prev_ref[...])
    ext_sc[HALO:, :] = glu_ref[...]
    _shift_copies(ext_sc, sh_sc)
    for r0 in range(0, tm, CONV_ROWS):
        y_sc[r0:r0 + CONV_ROWS, :] = _conv_rows(ext_sc, sh_sc, cw_ref, r0)
    o_ref[...] = _mix_tail(y_sc[...], h_ref[...], a_ref[...], cb_ref, lng_ref, lnb_ref, cg_ref,
                           woa_ref, woc_ref)


def _mix_prompt(h, a_n, glu, first_halo, mw, *, batch, seq, tm):
    nt = seq // tm
    d = h.shape[1]
    hb = tm // HALO
    row = lambda b, i: (b * nt + i, 0)
    return pl.pallas_call(
        functools.partial(_mix_prompt_kernel, tm=tm),
        grid=(batch, nt),
        in_specs=[
            pl.BlockSpec((tm, d), row),
            pl.BlockSpec((tm, ATTN_WIDTH), row),
            pl.BlockSpec((tm, CONV_CH), row),
            pl.BlockSpec((HALO, CONV_CH), lambda b, i: (jnp.maximum((b * nt + i) * hb - 1, 0), 0)),
            _const_spec((HALO, CONV_CH)),
            _const_spec(mw["conv_w"].shape), _const_spec(mw["conv_b"].shape),
            _const_spec(mw["ln_g"].shape), _const_spec(mw["ln_b"].shape),
            _const_spec(mw["conv_grp"].shape),
            _const_spec(mw["w_out_a"].shape), _const_spec(mw["w_out_c"].shape),
        ],
        out_specs=pl.BlockSpec((tm, d), row),
        out_shape=jax.ShapeDtypeStruct(h.shape, _F32),
        scratch_shapes=[pltpu.VMEM((HALO + tm, CONV_CH), _F32),
                        pltpu.VMEM((SUBLANES - 1, HALO + tm - SUBLANES, CONV_CH), _F32),
                        pltpu.VMEM((tm, CONV_CH), _F32)],
        compiler_params=pltpu.CompilerParams(
            dimension_semantics=("arbitrary", "arbitrary"), vmem_limit_bytes=VMEM_LIMIT),
        name="mix_prompt",
    )(h, a_n, glu, glu, first_halo, mw["conv_w"], mw["conv_b"], mw["ln_g"], mw["ln_b"],
      mw["conv_grp"], mw["w_out_a"], mw["w_out_c"])


def _mix_sample_kernel(h_ref, a_ref, full_ref, cw_ref, cb_ref, lng_ref, lnb_ref,
                       cg_ref, woa_ref, woc_ref, o_ref, *, s_new):
    ys = []
    for t in range(s_new):
        y = jnp.zeros(full_ref.shape[1:], _F32)
        for k in range(CONV_WIDTH):
            y = y + full_ref[t + k] * cw_ref[k:k + 1, :]
        ys.append(y)
    y = jnp.concatenate(ys, axis=0)
    o_ref[...] = _mix_tail(y, h_ref[...], a_ref[...], cb_ref, lng_ref, lnb_ref, cg_ref, woa_ref, woc_ref)


def _mix_sample(h, a_n, full_t, mw, *, s_new):
    t, d = h.shape
    return pl.pallas_call(
        functools.partial(_mix_sample_kernel, s_new=s_new),
        grid=(1,),
        in_specs=[
            _const_spec((t, d)), _const_spec((t, ATTN_WIDTH)), _const_spec(full_t.shape),
            _const_spec(mw["conv_w"].shape), _const_spec(mw["conv_b"].shape),
            _const_spec(mw["ln_g"].shape), _const_spec(mw["ln_b"].shape),
            _const_spec(mw["conv_grp"].shape),
            _const_spec(mw["w_out_a"].shape), _const_spec(mw["w_out_c"].shape),
        ],
        out_specs=pl.BlockSpec((t, d), lambda i: (0, 0)),
        out_shape=jax.ShapeDtypeStruct((t, d), _F32),
        compiler_params=pltpu.CompilerParams(
            dimension_semantics=("arbitrary",), vmem_limit_bytes=VMEM_LIMIT),
        name="mix_sample",
    )(h, a_n, full_t, mw["conv_w"], mw["conv_b"], mw["ln_g"], mw["ln_b"],
      mw["conv_grp"], mw["w_out_a"], mw["w_out_c"])


def _rope_tables(pos):
    half = QK_ROPE_DIM // 2
    inv = ROPE_THETA ** (-jnp.arange(half, dtype=_F32) / half)
    ang = pos.astype(_F32)[:, None] * inv[None, :]
    cos = jnp.cos(ang)
    sin = jnp.sin(ang)
    n = pos.shape[0]
    cos2 = jnp.concatenate([cos, cos], axis=-1)
    sin2 = jnp.concatenate([sin, sin], axis=-1)
    pad_q = jnp.zeros((n, HEAD_SLOT - Q_HEAD_DIM), _F32)
    cosq = SCORE_SCALE * jnp.concatenate([jnp.ones((n, QK_NOPE_DIM), _F32), cos2, pad_q], axis=-1)
    sinq = SCORE_SCALE * jnp.concatenate([jnp.zeros((n, QK_NOPE_DIM), _F32), sin2, pad_q], axis=-1)
    pad_k = jnp.zeros((n, LANES - QK_ROPE_DIM), _F32)
    cosk = jnp.concatenate([cos2, pad_k], axis=-1)
    sink = jnp.concatenate([sin2, pad_k], axis=-1)
    return cosq, sinq, cosk, sink


def _rot_cols(w):
    half = QK_ROPE_DIM // 2
    return jnp.concatenate([-w[..., half:], w[..., :half]], axis=-1)


def _layer_weights(l, mix_norm, w_in, q_norm, w_uq, kv_norm, w_uk, w_uv):
    d = w_in.shape[1]
    win = w_in[l]
    o_kv = Q_RANK
    o_kr = o_kv + KV_RANK
    o_cv = o_kr + QK_ROPE_DIM
    w_kr = win[:, o_kr:o_cv]
    zpad = jnp.zeros((d, LANES - QK_ROPE_DIM), _F32)
    win_p = jnp.concatenate([win[:, :o_kr], win[:, o_cv:], w_kr, zpad, _rot_cols(w_kr), zpad], axis=-1)

    wq = w_uq[l].reshape(Q_RANK, N_HEADS, Q_HEAD_DIM)
    zq = jnp.zeros((Q_RANK, N_HEADS, HEAD_SLOT - Q_HEAD_DIM), _F32)
    w_qa = jnp.concatenate([wq, zq], axis=-1).reshape(Q_RANK, N_HEADS * HEAD_SLOT)
    w_qb = jnp.concatenate([jnp.zeros((Q_RANK, N_HEADS, QK_NOPE_DIM), _F32),
                            _rot_cols(wq[..., QK_NOPE_DIM:]), zq], axis=-1).reshape(Q_RANK, N_HEADS * HEAD_SLOT)

    wuk = w_uk[l]
    zk = jnp.zeros((KV_RANK, N_HEADS, HEAD_SLOT - QK_NOPE_DIM), _F32)
    w_k1 = jnp.concatenate([wuk, zk], axis=-1).reshape(KV_RANK, N_HEADS * HEAD_SLOT)
    eye = jnp.eye(QK_ROPE_DIM, dtype=_F32)
    place = jnp.concatenate([jnp.zeros((QK_ROPE_DIM, QK_NOPE_DIM), _F32), eye,
                             jnp.zeros((QK_ROPE_DIM, HEAD_SLOT - Q_HEAD_DIM), _F32)], axis=-1)
    w_v =w_uv[l].reshape(KV_RANK, ATTN_WIDTH)

    w_vt = jnp.concatenate([w_uv[l].transpose(1, 2, 0), jnp.zeros((N_HEADS, VT_ONES, KV_RANK), _F32)],
                           axis=1).reshape(N_HEADS * VT_ROWS, KV_RANK)
    vt_bias = jnp.tile(jnp.concatenate([jnp.zeros((V_HEAD_DIM, 1), _F32), jnp.ones((VT_ONES, 1), _F32)],
                                       axis=0), (N_HEADS, 1))

    head_eye = jnp.eye(N_HEADS, dtype=_F32)[:, None, :, None]
    slot_abs = jnp.concatenate([wuk.transpose(1, 2, 0),
                                jnp.zeros((N_HEADS, HEAD_SLOT - QK_NOPE_DIM, KV_RANK), _F32)], axis=1)
    w_abs = (head_eye * slot_abs[:, :, None, :]).reshape(N_HEADS * HEAD_SLOT, N_HEADS * KV_RANK)
    w_qr = (head_eye * place.T[None, :, None, :]).reshape(N_HEADS * HEAD_SLOT, N_HEADS * QK_ROPE_DIM)

    bf = lambda x: x.astype(_BF16)
    return {
        "mix_norm": mix_norm[l][None], "w_in": bf(win_p), "q_norm": q_norm[l][None],
        "w_qa": bf(w_qa), "w_qb": bf(w_qb), "kv_norm": kv_norm[l][None],
        "w_k1": bf(w_k1), "w_v": bf(w_v), "w_vt": bf(w_vt), "vt_bias": vt_bias,
        "w_abs": bf(w_abs), "w_qr": bf(w_qr),
    }


def _pick_tile(n, prefer):
    for t in prefer:
        if n % t == 0:
            return t
    return n


def kernel(x_prompt, x_sample, cache_kv_latent, cache_k_rope, state_conv, page_table, meta_tokens,
           ffn1_norm, ffn1_w_gate, ffn1_w_up, ffn1_w_down, mix_norm, w_in, q_norm, w_uq, kv_norm,
           w_uk, w_uv, conv_w, conv_b, conv_ln_g, conv_ln_b, attn_grp_norm, conv_grp_norm, w_out,
           ffn2_norm, ffn2_w_gate, ffn2_w_up, ffn2_w_down, final_norm):
    b_p, s_p, d = x_prompt.shape
    b_s, s_s, _ = x_sample.shape
    depth = ffn1_norm.shape[0]
    assert depth == 1, "the meta rows' later-layer inputs are not carried; single layer only"
    n_pages = page_table.shape[1]
    past_len = n_pages * PAGE_SIZE
    assert CONV_WIDTH - 1 <= HALO and N_META <= HALO
    l = 0
    bf = lambda x: x.astype(_BF16)

    pw = _layer_weights(l, mix_norm, w_in, q_norm, w_uq, kv_norm, w_uk, w_uv)
    f1 = (ffn1_norm[l][None], bf(ffn1_w_gate[l]), bf(ffn1_w_up[l]), bf(ffn1_w_down[l]))
    f2 = (ffn2_norm[l][None], bf(ffn2_w_gate[l]), bf(ffn2_w_up[l]), bf(ffn2_w_down[l]))
    fin = final_norm[None]
    mw = {
        "conv_w": jnp.concatenate([conv_w[l], jnp.zeros((1, CONV_CH), _F32)], axis=0),
        "conv_b": conv_b[l][None], "ln_g": conv_ln_g[l][None], "ln_b": conv_ln_b[l][None],
        "conv_grp": conv_grp_norm[l][None],
        "w_out_a": bf(w_out[l][:ATTN_WIDTH]), "w_out_c": bf(w_out[l][ATTN_WIDTH:]),
    }

    tm_p = _pick_tile(s_p, (512, 256, 128))
    tq = _pick_tile(s_p, (1024, 512, 256, 128))
    tk = _pick_tile(tq, (256, 128))
    assert tm_p % CONV_ROWS == 0 and tm_p % HALO == 0

    h_m = _ffn(meta_tokens, *f1, fin, tm=N_META, final=False)
    tab_m = _rope_tables(jnp.arange(N_META, dtype=jnp.int32))
    ckv_m, kr_m, glu_m, _, k_m, vt_m = _proj(h_m, tab_m, 1, pw, tm=N_META, absorbed=False, tkv=N_META)
    first_halo = jnp.concatenate([jnp.zeros((HALO - N_META, CONV_CH), _F32), glu_m], axis=0)

    xp = x_prompt.reshape(b_p * s_p, d)
    h_p = _ffn(xp, *f1, fin, tm=tm_p, final=False)
    tab_p = _rope_tables(N_META + jnp.arange(s_p, dtype=jnp.int32))
    tm_proj = _pick_tile(s_p, (1024, 512, 256, 128))
    ckv_p, kr_p, glu_p, q_p, k_p, vt_p = _proj(h_p, tab_p, s_p // tm_proj, pw, tm=tm_proj, absorbed=False, tkv=tk)
    a_p = _prompt_attention(q_p, k_p, vt_p, k_m, vt_m[0], attn_grp_norm[l][:, None],
                            batch=b_p, seq=s_p, tq=tq, tk=tk)
    h2_p = _mix_prompt(h_p, a_p, glu_p, first_halo, mw, batch=b_p, seq=s_p, tm=tm_p)
    y_p = _ffn(h2_p, *f2, fin, tm=tm_p, final=True).reshape(b_p, s_p, d)

    n_s = b_s * s_s
    xs = x_sample.transpose(1, 0, 2).reshape(n_s, d)
    h_s = _ffn(xs, *f1, fin, tm=n_s, final=False)
    pos_s = past_len + jnp.repeat(jnp.arange(s_s, dtype=jnp.int32), b_s)
    ckv_s, kr_s, glu_s, ql_s, qr_s = _proj(h_s, _rope_tables(pos_s), 1, pw, tm=n_s, absorbed=True)
    seq_major = lambda x: x.reshape(s_s, b_s, -1).transpose(1, 0, 2)
    ckv_sb, kr_sb, glu_sb = seq_major(ckv_s), seq_major(kr_s), seq_major(glu_s)
    ql_b = seq_major(ql_s).reshape(b_s, s_s * N_HEADS, KV_RANK)
    qr_b = seq_major(qr_s).reshape(b_s, s_s * N_HEADS, QK_ROPE_DIM)
    per_step = _pick_tile(n_pages, (128, 64, 32, 16, 8, 4, 2))
    chains = _pick_tile(per_step, (4, 2))
    a_s = _sample_attention(page_table, ql_b, qr_b, ckv_sb, kr_sb, pw["w_v"], attn_grp_norm[l][None],
                            cache_kv_latent[l], jnp.swapaxes(cache_k_rope[l], 1, 2),
                            pages=per_step // chains, chains=chains)
    a_st = a_s.transpose(1, 0, 2).reshape(n_s, ATTN_WIDTH)
    full_t = jnp.concatenate([state_conv[l].transpose(1, 0, 2), glu_s.reshape(s_s, b_s, CONV_CH)], axis=0)
    h2_s = _mix_sample(h_s, a_st, full_t, mw, s_new=s_s)
    y_s = _ffn(h2_s, *f2, fin, tm=n_s, final=True).reshape(s_s, b_s, d).transpose(1, 0, 2)

    bcast = lambda x: jnp.broadcast_to(x[None], (b_p,) + x.shape)
    new_ckv_p = jnp.concatenate([bcast(ckv_m), ckv_p.reshape(b_p, s_p, KV_RANK)], axis=1)[None]
    new_kr_p = jnp.concatenate([bcast(kr_m), kr_p.reshape(b_p, s_p, QK_ROPE_DIM)], axis=1)[None]
    n_keep = CONV_WIDTH - 1
    new_conv_p = glu_p.reshape(b_p, s_p, CONV_CH)[:, s_p - n_keep:][None]
    new_conv_s = jnp.concatenate([state_conv[l], glu_sb], axis=1)[:, -n_keep:][None]
    return (y_p, y_s, new_ckv_p, new_kr_p, new_conv_p, ckv_sb[None], kr_sb[None], new_conv_s)
```
